```python
import jax, jax.numpy as jnp
from jax import lax
import numpy as np

D_MODEL = 2048
BATCH = 16
SEQ = 2048
DEPTH = 4

D_MIX = D_MODEL
CONV_CH = D_MIX // 2
CONV_GROUPS = 8
CONV_WIDTH = 31
DN_HEADS = 8
DN_HEAD_DIM = (D_MIX - CONV_CH) // DN_HEADS
DN_DIM = DN_HEADS * DN_HEAD_DIM
SHORT_CONV = 4
CHUNK = 64
D_FF = 4 * D_MODEL
N_MOD = 6
EPS = 1e-6
IN_COLS = 2 * CONV_CH + 4 * DN_DIM + 2 * DN_HEADS

kernel_name = "hymba_style_conformer_gdn_hybrid"


def rmsnorm(x, w):
    xf = x.astype(jnp.float32)
    y = xf * lax.rsqrt(jnp.mean(xf * xf, axis=-1, keepdims=True) + EPS)
    return (y * w.astype(jnp.float32)).astype(x.dtype)


def layernorm(x, w, b):
    xf = x.astype(jnp.float32)
    mu = jnp.mean(xf, axis=-1, keepdims=True)
    xc = xf - mu
    y = xc * lax.rsqrt(jnp.mean(xc * xc, axis=-1, keepdims=True) + EPS)
    return (y * w.astype(jnp.float32) + b.astype(jnp.float32)).astype(x.dtype)


def l2norm(x):
    return x * lax.rsqrt(jnp.sum(x * x, axis=-1, keepdims=True) + EPS)


def causal_depthwise_conv(x, w):
    K, C = w.shape
    return lax.conv_general_dilated(
        x, w[:, None, :].astype(x.dtype), window_strides=(1,), padding=[(K - 1, 0)],
        dimension_numbers=("NWC", "WIO", "NWC"), feature_group_count=C)


def gated_delta_rule_chunked(q, k, v, g, beta):
    B, S, H, Dk = q.shape
    Dv = v.shape[-1]
    NC = S // CHUNK

    def to_chunks(t):
        return t.reshape(B, NC, CHUNK, H, t.shape[-1]).transpose(0, 3, 1, 2, 4)

    q = to_chunks(q) * (Dk ** -0.5)
    k = to_chunks(k)
    v = to_chunks(v)
    beta = beta.reshape(B, NC, CHUNK, H).transpose(0, 3, 1, 2)
    g = jnp.cumsum(g.reshape(B, NC, CHUNK, H).transpose(0, 3, 1, 2), axis=-1)

    k_beta = k * beta[..., None]
    v_beta = v * beta[..., None]
    causal = jnp.tril(jnp.ones((CHUNK, CHUNK), dtype=bool))
    strict = jnp.tril(jnp.ones((CHUNK, CHUNK), dtype=bool), -1)
    decay = jnp.exp(jnp.where(causal, g[..., :, None] - g[..., None, :], -jnp.inf))

    L = jnp.where(strict, jnp.einsum("bhncd,bhnmd->bhncm", k_beta, k) * decay, 0.0)
    A = L + jnp.eye(CHUNK, dtype=jnp.float32)
    rhs = jnp.concatenate([v_beta, k_beta * jnp.exp(g)[..., None]], axis=-1)
    sol = lax.linalg.triangular_solve(A, rhs, left_side=True, lower=True, unit_diagonal=True)
    u = sol[..., :Dv]
    w = sol[..., Dv:]

    attn = jnp.where(causal, jnp.einsum("bhncd,bhnmd->bhncm", q, k) * decay, 0.0)
    q_dec = q * jnp.exp(g)[..., None]
    k_dec = k * jnp.exp(g[..., -1:] - g)[..., None]
    g_last = jnp.exp(g[..., -1])

    xs = tuple(jnp.moveaxis(t, 2, 0) for t in (q_dec, k_dec, u, w, attn, g_last))

    def step(state, inp):
        qd, kd, u_c, w_c, a_c, gl = inp
        v_new = u_c - jnp.einsum("bhck,bhkv->bhcv", w_c, state)
        o = (jnp.einsum("bhck,bhkv->bhcv", qd, state)
             + jnp.einsum("bhcm,bhmv->bhcv", a_c, v_new))
        state = state * gl[..., None, None] + jnp.einsum("bhck,bhcv->bhkv", kd, v_new)
        return state, o

    s0 = jnp.zeros((B, H, Dk, Dv), jnp.float32)
    _, o = lax.scan(step, s0, xs)
    return o.transpose(1, 0, 3, 2, 4).reshape(B, S, H, Dv)


def modulate(h, shift, scale):
    return h * (1.0 + scale[:, None, :]) + shift[:, None, :]


def hybrid_layer(x, mod, norm1_w, w_in, conv_dw_w, conv_dw_b, conv_ln_w, conv_ln_b, w_pw2,
                 conv_out_norm_w, qkv_conv_w, a_log, dt_bias, dn_norm_w, w_out,
                 norm2_w, w_up, w_down):
    B, S, _ = x.shape
    shift1, scale1, gate1, shift2, scale2, gate2 = jnp.split(mod, N_MOD, axis=-1)

    h = modulate(rmsnorm(x, norm1_w), shift1, scale1)
    proj = h @ w_in
    cuts = np.cumsum([CONV_CH, CONV_CH, DN_DIM, DN_DIM, DN_DIM, DN_DIM, DN_HEADS])
    c_val, c_gate, q, k, v, z, b_logit, a_logit = jnp.split(proj, cuts, axis=-1)

    u = c_val * jax.nn.sigmoid(c_gate)
    u = causal_depthwise_conv(u, conv_dw_w) + conv_dw_b
    u = jax.nn.silu(layernorm(u, conv_ln_w, conv_ln_b))
    u = u @ w_pw2
    y_conv = rmsnorm(u, conv_out_norm_w)

    qkv = jax.nn.silu(causal_depthwise_conv(jnp.concatenate([q, k, v], axis=-1), qkv_conv_w))
    q, k, v = jnp.split(qkv.astype(jnp.float32), 3, axis=-1)
    q = l2norm(q.reshape(B, S, DN_HEADS, DN_HEAD_DIM))
    k = l2norm(k.reshape(B, S, DN_HEADS, DN_HEAD_DIM))
    v = v.reshape(B, S, DN_HEADS, DN_HEAD_DIM)
    beta = jax.nn.sigmoid(b_logit.astype(jnp.float32))
    g = -jnp.exp(a_log.astype(jnp.float32)) * jax.nn.softplus(
        a_logit.astype(jnp.float32) + dt_bias.astype(jnp.float32))
    o = gated_delta_rule_chunked(q, k, v, g, beta)
    o = rmsnorm(o, dn_norm_w) * jax.nn.silu(
        z.astype(jnp.float32).reshape(B, S, DN_HEADS, DN_HEAD_DIM))
    y_dn = o.reshape(B, S, DN_DIM).astype(x.dtype)

    y = jnp.concatenate([y_conv, y_dn], axis=-1) @ w_out
    x = x + gate1[:, None, :] * y

    h2 = modulate(rmsnorm(x, norm2_w), shift2, scale2)
    m = jnp.square(jax.nn.relu(h2 @ w_up)) @ w_down
    return x + gate2[:, None, :] * m


def _fwd_setup_inputs(seed: int = 0) -> dict:
    key = jax.random.key(seed)
    ks = jax.random.split(key, 24)
    f32 = jnp.float32
    nrm = lambda k, shape, s: jax.random.normal(k, shape, f32) * s
    dt = jnp.exp(jax.random.uniform(ks[15], (DEPTH, DN_HEADS), f32, np.log(1e-3), np.log(1e-1)))
    return {
        "x": nrm(ks[0], (BATCH, SEQ, D_MODEL), 1.0),
        "c": nrm(ks[1], (BATCH, D_MODEL), 1.0),
        "w_ada": nrm(ks[2], (DEPTH, D_MODEL, N_MOD * D_MODEL), 0.5 * D_MODEL ** -0.5),
        "b_ada": nrm(ks[3], (DEPTH, N_MOD * D_MODEL), 0.01),
        "norm1_w": 1.0 + nrm(ks[4], (DEPTH, D_MODEL), 0.02),
        "w_in": nrm(ks[5], (DEPTH, D_MODEL, IN_COLS), D_MODEL ** -0.5),
        "conv_dw_w": nrm(ks[6], (DEPTH, CONV_WIDTH, CONV_CH), CONV_WIDTH ** -0.5),
        "conv_dw_b": nrm(ks[7], (DEPTH, CONV_CH), 0.02),
        "conv_ln_w": 1.0 + nrm(ks[8], (DEPTH, CONV_CH), 0.02),
        "conv_ln_b": nrm(ks[9], (DEPTH, CONV_CH), 0.02),
        "w_pw2": nrm(ks[10], (DEPTH, CONV_CH, CONV_CH), CONV_CH ** -0.5),
        "conv_out_norm_w": 1.0 + nrm(ks[11], (DEPTH, CONV_CH), 0.02),
        "qkv_conv_w": nrm(ks[12], (DEPTH, SHORT_CONV, 3 * DN_DIM), SHORT_CONV ** -0.5),
        "a_log": jnp.log(jax.random.uniform(ks[13], (DEPTH, DN_HEADS), f32, 1.0, 16.0)),
        "dt_bias": dt + jnp.log(-jnp.expm1(-dt)),
        "dn_norm_w": 1.0 + nrm(ks[14], (DEPTH, DN_HEAD_DIM), 0.02),
        "w_out": nrm(ks[16], (DEPTH, D_MIX, D_MODEL), D_MIX ** -0.5),
        "norm2_w": 1.0 + nrm(ks[17], (DEPTH, D_MODEL), 0.02),
        "w_up": nrm(ks[18], (DEPTH, D_MODEL, D_FF), D_MODEL ** -0.5),
        "w_down": nrm(ks[19], (DEPTH, D_FF, D_MODEL), D_FF ** -0.5),
        "final_ada_w": nrm(ks[20], (D_MODEL, 2 * D_MODEL), 0.5 * D_MODEL ** -0.5),
        "final_ada_b": nrm(ks[21], (2 * D_MODEL,), 0.01),
        "final_norm_w": 1.0 + nrm(ks[22], (D_MODEL,), 0.02),
    }


def _fwd_reference(x, c, w_ada, b_ada, norm1_w, w_in, conv_dw_w, conv_dw_b, conv_ln_w, conv_ln_b,
              w_pw2, conv_out_norm_w, qkv_conv_w, a_log, dt_bias, dn_norm_w, w_out,
              norm2_w, w_up, w_down, final_ada_w, final_ada_b, final_norm_w):
    c_act = jax.nn.silu(c)
    for l in range(DEPTH):
        mod = c_act @ w_ada[l] + b_ada[l]
        x = hybrid_layer(x, mod, norm1_w[l], w_in[l], conv_dw_w[l], conv_dw_b[l],
                         conv_ln_w[l], conv_ln_b[l], w_pw2[l], conv_out_norm_w[l],
                         qkv_conv_w[l], a_log[l], dt_bias[l], dn_norm_w[l], w_out[l],
                         norm2_w[l], w_up[l], w_down[l])
    shift_f, scale_f = jnp.split(c_act @ final_ada_w + final_ada_b, 2, axis=-1)
    return modulate(rmsnorm(x, final_norm_w), shift_f, scale_f)


import jax as _jax
import jax.numpy as _jnp

TWIN_FORMAT = 'train_step'
FWD_PARAMS = ['x', 'c', 'w_ada', 'b_ada', 'norm1_w', 'w_in', 'conv_dw_w', 'conv_dw_b', 'conv_ln_w', 'conv_ln_b', 'w_pw2', 'conv_out_norm_w', 'qkv_conv_w', 'a_log', 'dt_bias', 'dn_norm_w', 'w_out', 'norm2_w', 'w_up', 'w_down', 'final_ada_w', 'final_ada_b', 'final_norm_w']
TWIN_WEIGHTS = ['w_ada', 'b_ada', 'norm1_w', 'w_in', 'conv_dw_w', 'conv_dw_b', 'conv_ln_w', 'conv_ln_b', 'w_pw2', 'conv_out_norm_w', 'qkv_conv_w', 'a_log', 'dt_bias', 'dn_norm_w', 'w_out', 'norm2_w', 'w_up', 'w_down', 'final_ada_w', 'final_ada_b', 'final_norm_w']
TWIN_DIFF_INPUT = 'x'
TWIN_INPUTS = ['x', 'c', 'w_ada', 'b_ada', 'norm1_w', 'w_in', 'conv_dw_w', 'conv_dw_b', 'conv_ln_w', 'conv_ln_b', 'w_pw2', 'conv_out_norm_w', 'qkv_conv_w', 'a_log', 'dt_bias', 'dn_norm_w', 'w_out', 'norm2_w', 'w_up', 'w_down', 'final_ada_w', 'final_ada_b', 'final_norm_w', 'loss_target', 'm_w_ada', 'm_b_ada', 'm_norm1_w', 'm_w_in', 'm_conv_dw_w', 'm_conv_dw_b', 'm_conv_ln_w', 'm_conv_ln_b', 'm_w_pw2', 'm_conv_out_norm_w', 'm_qkv_conv_w', 'm_a_log', 'm_dt_bias', 'm_dn_norm_w', 'm_w_out', 'm_norm2_w', 'm_w_up', 'm_w_down', 'm_final_ada_w', 'm_final_ada_b', 'm_final_norm_w', 'v_w_ada', 'v_b_ada', 'v_norm1_w', 'v_w_in', 'v_conv_dw_w', 'v_conv_dw_b', 'v_conv_ln_w', 'v_conv_ln_b', 'v_w_pw2', 'v_conv_out_norm_w', 'v_qkv_conv_w', 'v_a_log', 'v_dt_bias', 'v_dn_norm_w', 'v_w_out', 'v_norm2_w', 'v_w_up', 'v_w_down', 'v_final_ada_w', 'v_final_ada_b', 'v_final_norm_w']
TWIN_OUTPUTS = ['loss', 'grad_x', 'grad_w_ada', 'grad_b_ada', 'grad_norm1_w', 'grad_w_in', 'grad_conv_dw_w', 'grad_conv_dw_b', 'grad_conv_ln_w', 'grad_conv_ln_b', 'grad_w_pw2', 'grad_conv_out_norm_w', 'grad_qkv_conv_w', 'grad_a_log', 'grad_dt_bias', 'grad_dn_norm_w', 'grad_w_out', 'grad_norm2_w', 'grad_w_up', 'grad_w_down', 'grad_final_ada_w', 'grad_final_ada_b', 'grad_final_norm_w', 'delta_w_ada', 'delta_b_ada', 'delta_norm1_w', 'delta_w_in', 'delta_conv_dw_w', 'delta_conv_dw_b', 'delta_conv_ln_w', 'delta_conv_ln_b', 'delta_w_pw2', 'delta_conv_out_norm_w', 'delta_qkv_conv_w', 'delta_a_log', 'delta_dt_bias', 'delta_dn_norm_w', 'delta_w_out', 'delta_norm2_w', 'delta_w_up', 'delta_w_down', 'delta_final_ada_w', 'delta_final_ada_b', 'delta_final_norm_w', 'new_m_w_ada', 'new_m_b_ada', 'new_m_norm1_w', 'new_m_w_in', 'new_m_conv_dw_w', 'new_m_conv_dw_b', 'new_m_conv_ln_w', 'new_m_conv_ln_b', 'new_m_w_pw2', 'new_m_conv_out_norm_w', 'new_m_qkv_conv_w', 'new_m_a_log', 'new_m_dt_bias', 'new_m_dn_norm_w', 'new_m_w_out', 'new_m_norm2_w', 'new_m_w_up', 'new_m_w_down', 'new_m_final_ada_w', 'new_m_final_ada_b', 'new_m_final_norm_w', 'new_v_w_ada', 'new_v_b_ada', 'new_v_norm1_w', 'new_v_w_in', 'new_v_conv_dw_w', 'new_v_conv_dw_b', 'new_v_conv_ln_w', 'new_v_conv_ln_b', 'new_v_w_pw2', 'new_v_conv_out_norm_w', 'new_v_qkv_conv_w', 'new_v_a_log', 'new_v_dt_bias', 'new_v_dn_norm_w', 'new_v_w_out', 'new_v_norm2_w', 'new_v_w_up', 'new_v_w_down', 'new_v_final_ada_w', 'new_v_final_ada_b', 'new_v_final_norm_w']
TWIN_LEAF_KINDS = {'loss': 'loss', 'grad_x': 'grad_x', 'grad_w_ada': 'grad_w', 'grad_b_ada': 'grad_w', 'grad_norm1_w': 'grad_w', 'grad_w_in': 'grad_w', 'grad_conv_dw_w': 'grad_w', 'grad_conv_dw_b': 'grad_w', 'grad_conv_ln_w': 'grad_w', 'grad_conv_ln_b': 'grad_w', 'grad_w_pw2': 'grad_w', 'grad_conv_out_norm_w': 'grad_w', 'grad_qkv_conv_w': 'grad_w', 'grad_a_log': 'grad_w', 'grad_dt_bias': 'grad_w', 'grad_dn_norm_w': 'grad_w', 'grad_w_out': 'grad_w', 'grad_norm2_w': 'grad_w', 'grad_w_up': 'grad_w', 'grad_w_down': 'grad_w', 'grad_final_ada_w': 'grad_w', 'grad_final_ada_b': 'grad_w', 'grad_final_norm_w': 'grad_w', 'delta_w_ada': 'delta_w', 'delta_b_ada': 'delta_w', 'delta_norm1_w': 'delta_w', 'delta_w_in': 'delta_w', 'delta_conv_dw_w': 'delta_w', 'delta_conv_dw_b': 'delta_w', 'delta_conv_ln_w': 'delta_w', 'delta_conv_ln_b': 'delta_w', 'delta_w_pw2': 'delta_w', 'delta_conv_out_norm_w': 'delta_w', 'delta_qkv_conv_w': 'delta_w', 'delta_a_log': 'delta_w', 'delta_dt_bias': 'delta_w', 'delta_dn_norm_w': 'delta_w', 'delta_w_out': 'delta_w', 'delta_norm2_w': 'delta_w', 'delta_w_up': 'delta_w', 'delta_w_down': 'delta_w', 'delta_final_ada_w': 'delta_w', 'delta_final_ada_b': 'delta_w', 'delta_final_norm_w': 'delta_w', 'new_m_w_ada': 'new_m', 'new_m_b_ada': 'new_m', 'new_m_norm1_w': 'new_m', 'new_m_w_in': 'new_m', 'new_m_conv_dw_w': 'new_m', 'new_m_conv_dw_b': 'new_m', 'new_m_conv_ln_w': 'new_m', 'new_m_conv_ln_b': 'new_m', 'new_m_w_pw2': 'new_m', 'new_m_conv_out_norm_w': 'new_m', 'new_m_qkv_conv_w': 'new_m', 'new_m_a_log': 'new_m', 'new_m_dt_bias': 'new_m', 'new_m_dn_norm_w': 'new_m', 'new_m_w_out': 'new_m', 'new_m_norm2_w': 'new_m', 'new_m_w_up': 'new_m', 'new_m_w_down': 'new_m', 'new_m_final_ada_w': 'new_m', 'new_m_final_ada_b': 'new_m', 'new_m_final_norm_w': 'new_m', 'new_v_w_ada': 'new_v', 'new_v_b_ada': 'new_v', 'new_v_norm1_w': 'new_v', 'new_v_w_in': 'new_v', 'new_v_conv_dw_w': 'new_v', 'new_v_conv_dw_b': 'new_v', 'new_v_conv_ln_w': 'new_v', 'new_v_conv_ln_b': 'new_v', 'new_v_w_pw2': 'new_v', 'new_v_conv_out_norm_w': 'new_v', 'new_v_qkv_conv_w': 'new_v', 'new_v_a_log': 'new_v', 'new_v_dt_bias': 'new_v', 'new_v_dn_norm_w': 'new_v', 'new_v_w_out': 'new_v', 'new_v_norm2_w': 'new_v', 'new_v_w_up': 'new_v', 'new_v_w_down': 'new_v', 'new_v_final_ada_w': 'new_v', 'new_v_final_ada_b': 'new_v', 'new_v_final_norm_w': 'new_v'}


def _forward(args):
    return _fwd_reference(*[args[k] for k in FWD_PARAMS])


def _output_shape():
    out = _jax.eval_shape(lambda: _forward(_fwd_setup_inputs(0)))
    return out.shape, out.dtype

N_MICROBATCH = 1
ADAM_LR = 0.001
ADAM_B1 = 0.9
ADAM_B2 = 0.999
ADAM_EPS = 1e-08
ADAM_WD = 0.01
ADAM_STEP = 10
PER_EXAMPLE_BATCH_AXIS = {'x': 0, 'c': 0, 'loss_target': 0}
SHARED_INPUTS = []
_WEIGHT_DTYPES = {'w_ada': _jnp.float32, 'b_ada': _jnp.float32, 'norm1_w': _jnp.float32, 'w_in': _jnp.float32, 'conv_dw_w': _jnp.float32, 'conv_dw_b': _jnp.float32, 'conv_ln_w': _jnp.float32, 'conv_ln_b': _jnp.float32, 'w_pw2': _jnp.float32, 'conv_out_norm_w': _jnp.float32, 'qkv_conv_w': _jnp.float32, 'a_log': _jnp.float32, 'dt_bias': _jnp.float32, 'dn_norm_w': _jnp.float32, 'w_out': _jnp.float32, 'norm2_w': _jnp.float32, 'w_up': _jnp.float32, 'w_down': _jnp.float32, 'final_ada_w': _jnp.float32, 'final_ada_b': _jnp.float32, 'final_norm_w': _jnp.float32}
MOMENT_SCALE = {'w_ada': 5.959875e-01, 'b_ada': 1.038948e+00, 'norm1_w': 7.259856e-02, 'w_in': 7.736422e-02, 'conv_dw_w': 1.787277e-01, 'conv_dw_b': 9.503217e-01, 'conv_ln_w': 4.494386e-01, 'conv_ln_b': 6.071418e-01, 'w_pw2': 2.965995e-01, 'conv_out_norm_w': 2.956992e-01, 'qkv_conv_w': 5.061403e-02, 'a_log': 1.697477e-01, 'dt_bias': 1.655165e-01, 'dn_norm_w': 3.757634e-01, 'w_out': 2.253245e-01, 'norm2_w': 1.202025e-01, 'w_up': 1.043597e-01, 'w_down': 4.560479e-01, 'final_ada_w': 4.478593e+00, 'final_ada_b': 1.193114e+01, 'final_norm_w': 1.841064e+01}


def _to_microbatches(a, axis):
    t = _jnp.moveaxis(a, axis, 0)
    t = t.reshape((N_MICROBATCH, t.shape[0] // N_MICROBATCH) + t.shape[1:])
    return _jnp.moveaxis(t, 1, axis + 1)


def setup_inputs(seed: int = 0) -> dict:
    inp = _fwd_setup_inputs(seed)
    key = _jax.random.fold_in(_jax.random.key(seed), 7919)
    shape, _ = _output_shape()
    out = dict(inp)
    out["loss_target"] = _jax.random.normal(_jax.random.fold_in(key, 0), shape, _jnp.float32)
    for i, name in enumerate(TWIN_WEIGHTS):
        w = inp[name].astype(_jnp.float32)
        if MOMENT_SCALE is None:
            s = _jnp.sqrt(_jnp.mean(_jnp.square(w)) + 1e-30)
        else:
            s = MOMENT_SCALE[name]
        km, kv = _jax.random.split(_jax.random.fold_in(key, i + 1))
        out[name] = w
        out["m_" + name] = s * _jax.random.normal(km, w.shape, _jnp.float32)
        out["v_" + name] = (s * s) * _jax.random.uniform(kv, w.shape, _jnp.float32, 0.5, 1.5)
    if N_MICROBATCH > 1:
        for name, axis in PER_EXAMPLE_BATCH_AXIS.items():
            out[name] = _to_microbatches(out[name], axis)
    return {'x': out['x'], 'c': out['c'], 'w_ada': out['w_ada'], 'b_ada': out['b_ada'], 'norm1_w': out['norm1_w'], 'w_in': out['w_in'], 'conv_dw_w': out['conv_dw_w'], 'conv_dw_b': out['conv_dw_b'], 'conv_ln_w': out['conv_ln_w'], 'conv_ln_b': out['conv_ln_b'], 'w_pw2': out['w_pw2'], 'conv_out_norm_w': out['conv_out_norm_w'], 'qkv_conv_w': out['qkv_conv_w'], 'a_log': out['a_log'], 'dt_bias': out['dt_bias'], 'dn_norm_w': out['dn_norm_w'], 'w_out': out['w_out'], 'norm2_w': out['norm2_w'], 'w_up': out['w_up'], 'w_down': out['w_down'], 'final_ada_w': out['final_ada_w'], 'final_ada_b': out['final_ada_b'], 'final_norm_w': out['final_norm_w'], 'loss_target': out['loss_target'], 'm_w_ada': out['m_w_ada'], 'm_b_ada': out['m_b_ada'], 'm_norm1_w': out['m_norm1_w'], 'm_w_in': out['m_w_in'], 'm_conv_dw_w': out['m_conv_dw_w'], 'm_conv_dw_b': out['m_conv_dw_b'], 'm_conv_ln_w': out['m_conv_ln_w'], 'm_conv_ln_b': out['m_conv_ln_b'], 'm_w_pw2': out['m_w_pw2'], 'm_conv_out_norm_w': out['m_conv_out_norm_w'], 'm_qkv_conv_w': out['m_qkv_conv_w'], 'm_a_log': out['m_a_log'], 'm_dt_bias': out['m_dt_bias'], 'm_dn_norm_w': out['m_dn_norm_w'], 'm_w_out': out['m_w_out'], 'm_norm2_w': out['m_norm2_w'], 'm_w_up': out['m_w_up'], 'm_w_down': out['m_w_down'], 'm_final_ada_w': out['m_final_ada_w'], 'm_final_ada_b': out['m_final_ada_b'], 'm_final_norm_w': out['m_final_norm_w'], 'v_w_ada': out['v_w_ada'], 'v_b_ada': out['v_b_ada'], 'v_norm1_w': out['v_norm1_w'], 'v_w_in': out['v_w_in'], 'v_conv_dw_w': out['v_conv_dw_w'], 'v_conv_dw_b': out['v_conv_dw_b'], 'v_conv_ln_w': out['v_conv_ln_w'], 'v_conv_ln_b': out['v_conv_ln_b'], 'v_w_pw2': out['v_w_pw2'], 'v_conv_out_norm_w': out['v_conv_out_norm_w'], 'v_qkv_conv_w': out['v_qkv_conv_w'], 'v_a_log': out['v_a_log'], 'v_dt_bias': out['v_dt_bias'], 'v_dn_norm_w': out['v_dn_norm_w'], 'v_w_out': out['v_w_out'], 'v_norm2_w': out['v_norm2_w'], 'v_w_up': out['v_w_up'], 'v_w_down': out['v_w_down'], 'v_final_ada_w': out['v_final_ada_w'], 'v_final_ada_b': out['v_final_ada_b'], 'v_final_norm_w': out['v_final_norm_w']}


def _loss(weights, diff, rest, loss_target):
    with _jax.named_scope("forward"):
        args = {**rest, TWIN_DIFF_INPUT: diff, **{k: w.astype(_WEIGHT_DTYPES[k]) for k, w in weights.items()}}
        y = _forward(args)
    with _jax.named_scope("loss_head"):
        err = _jnp.square(y.astype(_jnp.float32) - loss_target)
        return 0.5 * _jnp.sum(_jnp.mean(err, axis=-1)) if err.ndim else 0.5 * err


def _adamw(w, g, m, v):
    m = ADAM_B1 * m + (1.0 - ADAM_B1) * g
    v = ADAM_B2 * v + (1.0 - ADAM_B2) * _jnp.square(g)
    m_hat = m / (1.0 - ADAM_B1 ** ADAM_STEP)
    v_hat = v / (1.0 - ADAM_B2 ** ADAM_STEP)
    delta = -ADAM_LR * (m_hat / (_jnp.sqrt(v_hat) + ADAM_EPS) + ADAM_WD * w)
    return delta, m, v


def reference(x, c, w_ada, b_ada, norm1_w, w_in, conv_dw_w, conv_dw_b, conv_ln_w, conv_ln_b, w_pw2, conv_out_norm_w, qkv_conv_w, a_log, dt_bias, dn_norm_w, w_out, norm2_w, w_up, w_down, final_ada_w, final_ada_b, final_norm_w, loss_target, m_w_ada, m_b_ada, m_norm1_w, m_w_in, m_conv_dw_w, m_conv_dw_b, m_conv_ln_w, m_conv_ln_b, m_w_pw2, m_conv_out_norm_w, m_qkv_conv_w, m_a_log, m_dt_bias, m_dn_norm_w, m_w_out, m_norm2_w, m_w_up, m_w_down, m_final_ada_w, m_final_ada_b, m_final_norm_w, v_w_ada, v_b_ada, v_norm1_w, v_w_in, v_conv_dw_w, v_conv_dw_b, v_conv_ln_w, v_conv_ln_b, v_w_pw2, v_conv_out_norm_w, v_qkv_conv_w, v_a_log, v_dt_bias, v_dn_norm_w, v_w_out, v_norm2_w, v_w_up, v_w_down, v_final_ada_w, v_final_ada_b, v_final_norm_w):
    given = dict(x=x, c=c, w_ada=w_ada, b_ada=b_ada, norm1_w=norm1_w, w_in=w_in, conv_dw_w=conv_dw_w, conv_dw_b=conv_dw_b, conv_ln_w=conv_ln_w, conv_ln_b=conv_ln_b, w_pw2=w_pw2, conv_out_norm_w=conv_out_norm_w, qkv_conv_w=qkv_conv_w, a_log=a_log, dt_bias=dt_bias, dn_norm_w=dn_norm_w, w_out=w_out, norm2_w=norm2_w, w_up=w_up, w_down=w_down, final_ada_w=final_ada_w, final_ada_b=final_ada_b, final_norm_w=final_norm_w, loss_target=loss_target, m_w_ada=m_w_ada, m_b_ada=m_b_ada, m_norm1_w=m_norm1_w, m_w_in=m_w_in, m_conv_dw_w=m_conv_dw_w, m_conv_dw_b=m_conv_dw_b, m_conv_ln_w=m_conv_ln_w, m_conv_ln_b=m_conv_ln_b, m_w_pw2=m_w_pw2, m_conv_out_norm_w=m_conv_out_norm_w, m_qkv_conv_w=m_qkv_conv_w, m_a_log=m_a_log, m_dt_bias=m_dt_bias, m_dn_norm_w=m_dn_norm_w, m_w_out=m_w_out, m_norm2_w=m_norm2_w, m_w_up=m_w_up, m_w_down=m_w_down, m_final_ada_w=m_final_ada_w, m_final_ada_b=m_final_ada_b, m_final_norm_w=m_final_norm_w, v_w_ada=v_w_ada, v_b_ada=v_b_ada, v_norm1_w=v_norm1_w, v_w_in=v_w_in, v_conv_dw_w=v_conv_dw_w, v_conv_dw_b=v_conv_dw_b, v_conv_ln_w=v_conv_ln_w, v_conv_ln_b=v_conv_ln_b, v_w_pw2=v_w_pw2, v_conv_out_norm_w=v_conv_out_norm_w, v_qkv_conv_w=v_qkv_conv_w, v_a_log=v_a_log, v_dt_bias=v_dt_bias, v_dn_norm_w=v_dn_norm_w, v_w_out=v_w_out, v_norm2_w=v_norm2_w, v_w_up=v_w_up, v_w_down=v_w_down, v_final_ada_w=v_final_ada_w, v_final_ada_b=v_final_ada_b, v_final_norm_w=v_final_norm_w)
    weights = {n: given[n] for n in TWIN_WEIGHTS}
    shared = {n: given[n] for n in SHARED_INPUTS}
    per_example = {n: given[n] for n in ['x', 'c']}
    grad_fn = _jax.value_and_grad(_loss, argnums=(0, 1))

    def one_microbatch(ex, loss_target):
        ex = dict(ex)
        diff = ex.pop(TWIN_DIFF_INPUT)
        return grad_fn(weights, diff, {**shared, **ex}, loss_target)

    if N_MICROBATCH == 1:
        loss, (grad_w, grad_x) = one_microbatch(per_example, given["loss_target"])
    else:
        def body(carry, xs):
            loss_sum, grad_sum = carry
            l_k, (gw_k, gx_k) = one_microbatch(xs[0], xs[1])
            with _jax.named_scope("update"):
                return (loss_sum + l_k, _jax.tree.map(_jnp.add, grad_sum, gw_k)), gx_k

        init = (_jnp.zeros((), _jnp.float32), _jax.tree.map(_jnp.zeros_like, weights))
        (loss, grad_w), grad_x = _jax.lax.scan(body, init, (per_example, given["loss_target"]))
    with _jax.named_scope("update"):
        delta_w, new_m, new_v = {}, {}, {}
        for n in TWIN_WEIGHTS:
            delta_w[n], new_m[n], new_v[n] = _adamw(weights[n], grad_w[n], given["m_" + n], given["v_" + n])
    return (loss, grad_x, *[grad_w[n] for n in TWIN_WEIGHTS], *[delta_w[n] for n in TWIN_WEIGHTS],
            *[new_m[n] for n in TWIN_WEIGHTS], *[new_v[n] for n in TWIN_WEIGHTS])
```

```python
import functools

import numpy as np
import jax
import jax.numpy as jnp
from jax import lax
from jax.experimental import pallas as pl
from jax.experimental.pallas import tpu as pltpu

F32, BF16 = jnp.float32, jnp.bfloat16
EPS = 1e-6
CHUNK = 64
HEAD_DIM = 128
LANE = 128
SUBLANE = 8
N_DEV = 8
N_CHIP = 4
VMEM_LIMIT = 48 * 1024 * 1024
ADAM_LR, ADAM_B1, ADAM_B2, ADAM_EPS, ADAM_WD, ADAM_STEP = 0.001, 0.9, 0.999, 1e-08, 0.01, 10
MESH = pl.DeviceIdType.MESH
HI = lax.Precision.HIGHEST
NN, NT, TN = ((1,), (0,)), ((1,), (1,)), ((0,), (0,))


def _any_spec():
    return pl.BlockSpec(memory_space=pl.ANY)


def _cparams(*sem):
    return pltpu.CompilerParams(dimension_semantics=sem or None, vmem_limit_bytes=VMEM_LIMIT)


def _tile(dim, target, unit=LANE):
    best, t = None, unit
    while t <= min(dim, target):
        if dim % t == 0:
            best = t
        t += unit
    return best or dim


def _sigmoid(x):
    return jax.nn.sigmoid(x)


def _silu(x):
    return x * jax.nn.sigmoid(x)


def _softplus(x):
    return jnp.maximum(x, 0.0) + jnp.log(1.0 + jnp.exp(-jnp.abs(x)))


def _rms(x, w):
    return x * lax.rsqrt(jnp.mean(x * x, axis=-1, keepdims=True) + EPS) * w


def _matmul(a, b, mode, name, out_dtypes=(F32,), epi=None, sides=(), bias=None, precision=None,
            b_layer=None, tm=512, tn=1024, tk=1024):
    bshape = b.shape[1:] if b_layer is not None else b.shape
    if mode == "nn":
        (M, K), (K2, N) = a.shape, bshape
    elif mode == "nt":
        (M, K), (N, K2) = a.shape, bshape
    else:
        (K, M), (K2, N) = a.shape, bshape
    assert K == K2, (a.shape, b.shape, mode)
    tm, tn, tk = _tile(M, tm), _tile(N, tn), _tile(K, tk)
    nk = K // tk
    dims = {"nn": NN, "nt": NT, "tn": TN}[mode]
    n_side, n_out, has_bias = len(sides), len(out_dtypes), bias is not None

    if mode == "tn":
        a_spec = pl.BlockSpec((tk, tm), lambda i, j, k: (k, i))
    else:
        a_spec = pl.BlockSpec((tm, tk), lambda i, j, k: (i, k))
    if mode == "nt":
        bblk, bidx = (tn, tk), (lambda i, j, k: (j, k))
    else:
        bblk, bidx = (tk, tn), (lambda i, j, k: (k, j))
    if b_layer is not None:
        b_spec = pl.BlockSpec((None,) + bblk, lambda i, j, k: (b_layer,) + bidx(i, j, k))
    else:
        b_spec = pl.BlockSpec(bblk, bidx)
    mn_spec = pl.BlockSpec((tm, tn), lambda i, j, k: (i, j))
    in_specs = [a_spec, b_spec] + [mn_spec] * n_side
    if has_bias:
        in_specs.append(pl.BlockSpec((1, tn), lambda i, j, k: (0, j)))

    def body(*refs):
        a_ref, b_ref = refs[:2]
        side_refs = refs[2:2 + n_side]
        pos = 2 + n_side
        bias_ref = refs[pos] if has_bias else None
        pos += int(has_bias)
        out_refs, acc_ref = refs[pos:pos + n_out], refs[pos + n_out]
        k = pl.program_id(2)

        @pl.when(k == 0)
        def _():
            acc_ref[...] = jnp.zeros_like(acc_ref)

        av, bv = a_ref[...], b_ref[...]
        if precision is None:
            av, bv = av.astype(BF16), bv.astype(BF16)
        acc_ref[...] += lax.dot_general(av, bv, (dims, ((), ())), precision=precision,
                                        preferred_element_type=F32)

        @pl.when(k == nk - 1)
        def _():
            acc = acc_ref[...]
            if has_bias:
                acc = acc + bias_ref[...]
            outs = (acc,) if epi is None else epi(acc, *[s[...] for s in side_refs])
            for r, o in zip(out_refs, outs):
                r[...] = o.astype(r.dtype)

    res = pl.pallas_call(
        body, name=name, grid=(M // tm, N // tn, nk),
        in_specs=in_specs, out_specs=[mn_spec] * n_out,
        out_shape=[jax.ShapeDtypeStruct((M, N), d) for d in out_dtypes],
        scratch_shapes=[pltpu.VMEM((tm, tn), F32)],
        compiler_params=_cparams("parallel", "parallel", "arbitrary"),
    )(a, b, *sides, *([bias] if has_bias else []))
    return res[0] if n_out == 1 else res


def _row_specs(rows, bvecs, pvecs, tile, tpb):
    specs = [pl.BlockSpec((tile, w), lambda i, cb=cb: (i, cb)) for (_, cb, w) in rows]
    specs += [pl.BlockSpec((None, 1, v.shape[-1]), lambda i: (i // tpb, 0, 0)) for v in bvecs]
    specs += [pl.BlockSpec((1, v.shape[-1]), lambda i: (0, 0)) for v in pvecs]
    return specs


def _row_tiling(rows, bvecs, tile):
    T = rows[0][0].shape[0]
    Bl = bvecs[0].shape[0] if bvecs else 1
    tile = min(tile, T // Bl)
    assert (T // Bl) % tile == 0
    return T, tile, (T // Bl) // tile


def _rowwise(fn, rows, bvecs, pvecs, outs, name, tile=256):
    T, tile, tpb = _row_tiling(rows, bvecs, tile)
    n_in = len(rows) + len(bvecs) + len(pvecs)

    def body(*refs):
        vals = [r[...].astype(F32) for r in refs[:n_in]]
        res = fn(*vals)
        for r, o in zip(refs[n_in:], res):
            r[...] = o.astype(r.dtype)

    res = pl.pallas_call(
        body, name=name, grid=(T // tile,),
        in_specs=_row_specs(rows, bvecs, pvecs, tile, tpb),
        out_specs=[pl.BlockSpec((tile, w), lambda i: (i, 0)) for (w, _) in outs],
        out_shape=[jax.ShapeDtypeStruct((T, w), d) for (w, d) in outs],
        compiler_params=_cparams("parallel"),
    )(*[r[0] for r in rows], *bvecs, *pvecs)
    return res


def _rowwise_bwd(fn, rows, bvecs, pvecs, cots, row_grads, name, tile=128):
    T, tile, tpb = _row_tiling(rows, bvecs, tile)
    n_row, n_b, n_p, n_cot = len(rows), len(bvecs), len(pvecs), len(cots)
    n_in = n_row + n_b + n_p
    want = [k for k, d in enumerate(row_grads) if d is not None]

    def body(*refs):
        vals = [r[...].astype(F32) for r in refs[:n_in]]
        cot_vals = tuple(r[...].astype(F32) for r in refs[n_in:n_in + n_cot])
        out_refs = refs[n_in + n_cot:]
        _, vjp = jax.vjp(fn, *vals)
        g = vjp(cot_vals)
        i = pl.program_id(0)
        for r, k in zip(out_refs[:len(want)], want):
            r[...] = g[k].astype(r.dtype)
        pos = len(want)
        for q in range(n_b + n_p):
            ref, gv = out_refs[pos + q], g[n_row + q]
            first = (i % tpb == 0) if q < n_b else (i == 0)

            @pl.when(first)
            def _(ref=ref, gv=gv):
                ref[...] = gv

            @pl.when(jnp.logical_not(first))
            def _(ref=ref, gv=gv):
                ref[...] += gv

    out_specs = [pl.BlockSpec((tile, rows[k][2]), lambda i: (i, 0)) for k in want]
    out_specs += [pl.BlockSpec((None, 1, v.shape[-1]), lambda i: (i // tpb, 0, 0)) for v in bvecs]
    out_specs += [pl.BlockSpec((1, v.shape[-1]), lambda i: (0, 0)) for v in pvecs]
    out_shape = [jax.ShapeDtypeStruct((T, rows[k][2]), row_grads[k]) for k in want]
    out_shape += [jax.ShapeDtypeStruct(v.shape, F32) for v in bvecs]
    out_shape += [jax.ShapeDtypeStruct(v.shape, F32) for v in pvecs]
    res = pl.pallas_call(
        body, name=name, grid=(T // tile,),
        in_specs=_row_specs(rows, bvecs, pvecs, tile, tpb) + _row_specs(cots, [], [], tile, tpb),
        out_specs=out_specs, out_shape=out_shape,
        compiler_params=_cparams("arbitrary"),
    )(*[r[0] for r in rows], *bvecs, *pvecs, *[c[0] for c in cots])
    nw = len(want)
    return list(res[:nw]), list(res[nw:nw + n_b]), list(res[nw + n_b:])


def _f_first(x, shift, scale, w):
    return x, _rms(x, w) * (1.0 + scale) + shift


def _f_join(x1p, mp, gate2p, shift, scale, w):
    x = x1p + gate2p * mp
    return x, _rms(x, w) * (1.0 + scale) + shift


def _f_mid(x, y, gate1, shift2, scale2, w2):
    x1 = x + gate1 * y
    return x1, _rms(x1, w2) * (1.0 + scale2) + shift2


def _f_glu(val, gate):
    return (val * _sigmoid(gate),)


def _f_convmid(u1, b, lnw, lnb):
    u = u1 + b
    mu = jnp.mean(u, axis=-1, keepdims=True)
    xc = u - mu
    y = xc * lax.rsqrt(jnp.mean(xc * xc, axis=-1, keepdims=True) + EPS) * lnw + lnb
    return (_silu(y),)


def _f_rms(u, w):
    return (_rms(u, w),)


def _final_loss(x1, m, tgt, gate2, shift, scale, wf, name, tile=128):
    T, D = x1.shape
    Bl = gate2.shape[0]
    tile = min(tile, T // Bl)
    tpb = (T // Bl) // tile

    def body(x1_ref, m_ref, t_ref, g_ref, sh_ref, sc_ref, w_ref,
             dx1_ref, dm_ref, dg_ref, dsh_ref, dsc_ref, dw_ref, loss_ref):
        tg = t_ref[...]

        def lossfn(x1v, mv, g2, sh, sc, w):
            x2 = x1v + g2 * mv
            y = _rms(x2, w) * (1.0 + sc) + sh
            err = y - tg
            per_row = jnp.mean(err * err, axis=-1, keepdims=True)
            return 0.5 * jnp.sum(per_row, axis=0, keepdims=True)

        val, vjp = jax.vjp(lossfn, x1_ref[...], m_ref[...], g_ref[...], sh_ref[...], sc_ref[...], w_ref[...])
        g = vjp(jnp.ones((1, 1), F32))
        dx1_ref[...] = g[0]
        dm_ref[...] = g[1].astype(dm_ref.dtype)
        i = pl.program_id(0)
        acc = [(dg_ref, g[2], i % tpb == 0), (dsh_ref, g[3], i % tpb == 0), (dsc_ref, g[4], i % tpb == 0),
               (dw_ref, g[5], i == 0), (loss_ref, jnp.broadcast_to(val, (1, LANE)), i == 0)]
        for ref, gv, first in acc:
            @pl.when(first)
            def _(ref=ref, gv=gv):
                ref[...] = gv

            @pl.when(jnp.logical_not(first))
            def _(ref=ref, gv=gv):
                ref[...] += gv

    row = pl.BlockSpec((tile, D), lambda i: (i, 0))
    bv = pl.BlockSpec((None, 1, D), lambda i: (i // tpb, 0, 0))
    pv = pl.BlockSpec((1, D), lambda i: (0, 0))
    return pl.pallas_call(
        body, name=name, grid=(T // tile,),
        in_specs=[row, row, row, bv, bv, bv, pv],
        out_specs=[row, row, bv, bv, bv, pv, pl.BlockSpec((1, LANE), lambda i: (0, 0))],
        out_shape=[jax.ShapeDtypeStruct((T, D), F32), jax.ShapeDtypeStruct((T, D), BF16),
                   jax.ShapeDtypeStruct(gate2.shape, F32), jax.ShapeDtypeStruct(gate2.shape, F32),
                   jax.ShapeDtypeStruct(gate2.shape, F32), jax.ShapeDtypeStruct(wf.shape, F32),
                   jax.ShapeDtypeStruct((1, LANE), F32)],
        compiler_params=_cparams("arbitrary"),
    )(x1, m, tgt, gate2, shift, scale, wf)


CONV_ROWS = 128


def _conv_geometry(K):
    pad = -(-(K - 1) // SUBLANE) * SUBLANE
    return pad


def _fill_shifted(dst_ref, src_ref, shifts, length):
    for s in shifts:
        for r0 in range(0, length, CONV_ROWS):
            n = min(CONV_ROWS, length - r0)
            dst_ref[s, pl.ds(r0, n), :] = src_ref[pl.ds(r0 + s, n), :]


def _dwconv_fwd(x, xoff, w, woff, width, seq, name):
    T, K = x.shape[0], w.shape[0]
    CB = LANE
    P = _conv_geometry(K)
    nb, ncb = T // seq, width // CB
    xo, wo = xoff // CB, woff // CB
    offs = [P - (K - 1) + j for j in range(K)]
    shifts = sorted({o % SUBLANE for o in offs})
    RC = min(CONV_ROWS, seq)

    def body(x_ref, w_ref, o_ref, xpad, xs):
        xpad[pl.ds(0, P), :] = jnp.zeros((P, CB), F32)
        xpad[pl.ds(P, seq), :] = x_ref[...]
        xpad[pl.ds(P + seq, SUBLANE), :] = jnp.zeros((SUBLANE, CB), F32)
        _fill_shifted(xs, xpad, shifts, seq + P)
        wv = w_ref[...]

        def chunk(i, carry):
            r0 = pl.multiple_of(i * RC, RC)
            acc = jnp.zeros((RC, CB), F32)
            for j, o in enumerate(offs):
                acc = acc + wv[j:j + 1, :] * xs[o % SUBLANE, pl.ds(r0 + (o // SUBLANE) * SUBLANE, RC), :]
            o_ref[pl.ds(r0, RC), :] = acc
            return carry

        lax.fori_loop(0, seq // RC, chunk, 0)

    return pl.pallas_call(
        body, name=name, grid=(nb, ncb),
        in_specs=[pl.BlockSpec((seq, CB), lambda b, c: (b, xo + c)),
                  pl.BlockSpec((K, CB), lambda b, c: (0, wo + c))],
        out_specs=pl.BlockSpec((seq, CB), lambda b, c: (b, c)),
        out_shape=jax.ShapeDtypeStruct((T, width), F32),
        scratch_shapes=[pltpu.VMEM((seq + P + SUBLANE, CB), F32), pltpu.VMEM((SUBLANE, seq + P, CB), F32)],
        compiler_params=_cparams("parallel", "parallel"),
    )(x, w)


def _dwconv_bwd(x, xoff, w, woff, dy, width, seq, name, dx_dtype):
    T, K = x.shape[0], w.shape[0]
    CB = LANE
    P = _conv_geometry(K)
    nb, ncb = T // seq, width // CB
    xo, wo = xoff // CB, woff // CB
    offs = [P - (K - 1) + j for j in range(K)]
    roffs = [K - 1 - j for j in range(K)]
    xshifts = sorted({o % SUBLANE for o in offs})
    yshifts = sorted({o % SUBLANE for o in roffs})
    RC = min(CONV_ROWS, seq)

    def body(x_ref, w_ref, dy_ref, dx_ref, dw_ref, xpad, xs, ypad, ys, dwacc):
        b = pl.program_id(1)
        xpad[pl.ds(0, P), :] = jnp.zeros((P, CB), F32)
        xpad[pl.ds(P, seq), :] = x_ref[...]
        xpad[pl.ds(P + seq, SUBLANE), :] = jnp.zeros((SUBLANE, CB), F32)
        _fill_shifted(xs, xpad, xshifts, seq + P)
        ypad[pl.ds(0, seq), :] = dy_ref[...].astype(F32)
        ypad[pl.ds(seq, P + SUBLANE), :] = jnp.zeros((P + SUBLANE, CB), F32)
        _fill_shifted(ys, ypad, yshifts, seq + P)
        wv = w_ref[...]

        @pl.when(b == 0)
        def _():
            dwacc[...] = jnp.zeros_like(dwacc)

        def chunk(i, carry):
            r0 = pl.multiple_of(i * RC, RC)
            dyc = ypad[pl.ds(r0, RC), :]
            acc = jnp.zeros((RC, CB), F32)
            for j in range(K):
                o, ro = offs[j], roffs[j]
                acc = acc + wv[j:j + 1, :] * ys[ro % SUBLANE, pl.ds(r0 + (ro // SUBLANE) * SUBLANE, RC), :]
                prod = dyc * xs[o % SUBLANE, pl.ds(r0 + (o // SUBLANE) * SUBLANE, RC), :]
                part = prod[0:SUBLANE, :]
                for q in range(1, RC // SUBLANE):
                    part = part + prod[q * SUBLANE:(q + 1) * SUBLANE, :]
                dwacc[pl.ds(j * SUBLANE, SUBLANE), :] += part
            dx_ref[pl.ds(r0, RC), :] = acc.astype(dx_ref.dtype)
            return carry

        lax.fori_loop(0, seq // RC, chunk, 0)

        @pl.when(b == nb - 1)
        def _():
            for j in range(K):
                dw_ref[j:j + 1, :] = jnp.sum(dwacc[pl.ds(j * SUBLANE, SUBLANE), :], axis=0, keepdims=True)

    return pl.pallas_call(
        body, name=name, grid=(ncb, nb),
        in_specs=[pl.BlockSpec((seq, CB), lambda c, b: (b, xo + c)),
                  pl.BlockSpec((K, CB), lambda c, b: (0, wo + c)),
                  pl.BlockSpec((seq, CB), lambda c, b: (b, c))],
        out_specs=[pl.BlockSpec((seq, CB), lambda c, b: (b, c)),
                   pl.BlockSpec((K, CB), lambda c, b: (0, c))],
        out_shape=[jax.ShapeDtypeStruct((T, width), dx_dtype), jax.ShapeDtypeStruct((K, width), F32)],
        scratch_shapes=[pltpu.VMEM((seq + P + SUBLANE, CB), F32), pltpu.VMEM((SUBLANE, seq + P, CB), F32),
                        pltpu.VMEM((seq + P + SUBLANE, CB), F32), pltpu.VMEM((SUBLANE, seq + P, CB), F32),
                        pltpu.VMEM((K * SUBLANE, CB), F32)],
        compiler_params=_cparams("parallel", "arbitrary"),
    )(x, w, dy)


def _dot(a, b, dims):
    return lax.dot_general(a, b, (dims, ((), ())), precision=HI, preferred_element_type=F32)


def _delta_chunk(S, qr, kr, vr, z, lg, a_vec, dt_vec, nw, hidx, nheads):
    C, D = qr.shape
    lane = lax.broadcasted_iota(jnp.int32, (1, D), 1)
    sel_b, sel_a = lane == hidx, lane == hidx + nheads
    b_logit = jnp.sum(jnp.where(sel_b, lg, 0.0), axis=1, keepdims=True)
    a_logit = jnp.sum(jnp.where(sel_a, lg, 0.0), axis=1, keepdims=True)
    a_h = jnp.sum(jnp.where(sel_b, a_vec, 0.0), axis=1, keepdims=True)
    dt_h = jnp.sum(jnp.where(sel_b, dt_vec, 0.0), axis=1, keepdims=True)

    q, k, v = _silu(qr), _silu(kr), _silu(vr)
    q = q * lax.rsqrt(jnp.sum(q * q, axis=-1, keepdims=True) + EPS) * (D ** -0.5)
    k = k * lax.rsqrt(jnp.sum(k * k, axis=-1, keepdims=True) + EPS)
    beta = _sigmoid(b_logit)
    g = -jnp.exp(a_h) * _softplus(a_logit + dt_h)

    row = lax.broadcasted_iota(jnp.int32, (C, C), 0)
    col = lax.broadcasted_iota(jnp.int32, (C, C), 1)
    causal, strict, eye = row >= col, row > col, row == col
    tril = causal.astype(F32)
    gc_d = _dot(tril, jnp.broadcast_to(g, (C, D)), NN)
    gi = _dot(tril, jnp.broadcast_to(g, (C, C)), NN)
    gj = _dot(jnp.ones((C, C), F32), jnp.where(eye, gi, 0.0), NN)
    gtot = jnp.sum(g, axis=0, keepdims=True)
    decay = jnp.exp(jnp.where(causal, gi - gj, -1e30))
    e_gc = jnp.exp(gc_d)

    kb, vb = k * beta, v * beta
    L = jnp.where(strict, _dot(kb, k, NT) * decay, 0.0)
    Tm = jnp.where(eye, 1.0, 0.0) - L
    Mp = _dot(L, L, NN)
    n_sq = int(np.ceil(np.log2(C))) - 1
    for it in range(n_sq):
        Tm = Tm + _dot(Tm, Mp, NN)
        if it + 1 < n_sq:
            Mp = _dot(Mp, Mp, NN)
    u = _dot(Tm, vb, NN)
    w = _dot(Tm, kb * e_gc, NN)
    attn = jnp.where(causal, _dot(q, k, NT) * decay, 0.0)
    v_new = u - _dot(w, S, NN)
    o = _dot(q * e_gc, S, NN) + _dot(attn, v_new, NN)
    S_new = S * jnp.exp(gtot) + _dot(k * jnp.exp(gtot - gc_d), v_new, TN)
    y = _rms(o, nw) * _silu(z)
    return y, S_new


def _delta_specs(DN, CC, HP, NC, rev):
    W = HP * HEAD_DIM
    nb = (lambda n: NC - 1 - n) if rev else (lambda n: n)
    qkv = [pl.BlockSpec((CHUNK, W), lambda b, g, n, p=p: (b * NC + nb(n), p * (DN // W) + g)) for p in range(3)]
    z = pl.BlockSpec((CHUNK, W), lambda b, g, n: (b * NC + nb(n), (2 * CC + 3 * DN) // W + g))
    lg = pl.BlockSpec((CHUNK, LANE), lambda b, g, n: (b * NC + nb(n), (2 * CC + 4 * DN) // LANE))
    pv = pl.BlockSpec((1, LANE), lambda b, g, n: (0, 0))
    return qkv, z, lg, pv, nb


def _delta_fwd(qkv_c, proj, a_vec, dt_vec, nw, DN, CC, seq, name, HP):
    T = qkv_c.shape[0]
    H, NC, Bl = DN // HEAD_DIM, seq // CHUNK, T // seq
    W = HP * HEAD_DIM
    qkv, zs, lgs, pv, nb = _delta_specs(DN, CC, HP, NC, False)

    def body(q_ref, k_ref, v_ref, z_ref, lg_ref, a_ref, dt_ref, nw_ref, y_ref, ssave_ref, S_ref):
        n, grp = pl.program_id(2), pl.program_id(1)

        @pl.when(n == 0)
        def _():
            S_ref[...] = jnp.zeros_like(S_ref)

        lg, av, dv, nv = lg_ref[...], a_ref[...], dt_ref[...], nw_ref[...]
        for hh in range(HP):
            sl = slice(hh * HEAD_DIM, (hh + 1) * HEAD_DIM)
            S = S_ref[hh]
            ssave_ref[hh] = S
            y, S_new = _delta_chunk(S, q_ref[:, sl], k_ref[:, sl], v_ref[:, sl], z_ref[:, sl],
                                    lg, av, dv, nv, grp * HP + hh, H)
            y_ref[:, sl] = y.astype(y_ref.dtype)
            S_ref[hh] = S_new

    return pl.pallas_call(
        body, name=name, grid=(Bl, H // HP, NC),
        in_specs=[*qkv, zs, lgs, pv, pv, pv],
        out_specs=[pl.BlockSpec((CHUNK, W), lambda b, g, n: (b * NC + n, g)),
                   pl.BlockSpec((None, HP, None, HEAD_DIM, HEAD_DIM), lambda b, g, n: (b, g, n, 0, 0))],
        out_shape=[jax.ShapeDtypeStruct((T, DN), BF16),
                   jax.ShapeDtypeStruct((Bl, H, NC, HEAD_DIM, HEAD_DIM), F32)],
        scratch_shapes=[pltpu.VMEM((HP, HEAD_DIM, HEAD_DIM), F32)],
        compiler_params=_cparams("parallel", "parallel", "arbitrary"),
    )(qkv_c, qkv_c, qkv_c, proj, proj, a_vec, dt_vec, nw)


def _delta_bwd(qkv_c, proj, a_vec, dt_vec, nw, ssave, dy, dyoff, DN, CC, seq, name, HP):
    T = qkv_c.shape[0]
    H, NC, Bl = DN // HEAD_DIM, seq // CHUNK, T // seq
    W, G = HP * HEAD_DIM, DN // (HP * HEAD_DIM)
    qkv, zs, lgs, pv, nb = _delta_specs(DN, CC, HP, NC, True)

    def body(q_ref, k_ref, v_ref, z_ref, lg_ref, a_ref, dt_ref, nw_ref, ss_ref, dy_ref,
             dq_ref, dk_ref, dv_ref, dz_ref, dlg_ref, da_ref, ddt_ref, dnw_ref, dS_ref):
        b, grp, n = pl.program_id(0), pl.program_id(1), pl.program_id(2)

        @pl.when(n == 0)
        def _():
            dS_ref[...] = jnp.zeros_like(dS_ref)

        lg, av, dv, nv = lg_ref[...], a_ref[...], dt_ref[...], nw_ref[...]
        dlg = jnp.zeros((CHUNK, LANE), F32)
        dpar = [jnp.zeros((1, LANE), F32) for _ in range(3)]
        for hh in range(HP):
            sl = slice(hh * HEAD_DIM, (hh + 1) * HEAD_DIM)
            hidx = grp * HP + hh
            fn = functools.partial(_delta_chunk, hidx=hidx, nheads=H)
            _, vjp = jax.vjp(fn, ss_ref[hh], q_ref[:, sl], k_ref[:, sl], v_ref[:, sl], z_ref[:, sl], lg, av, dv, nv)
            gS, gq, gk, gv, gz, glg, ga, gdt, gnw = vjp((dy_ref[:, sl].astype(F32), dS_ref[hh]))
            dS_ref[hh] = gS
            dq_ref[:, sl] = gq.astype(dq_ref.dtype)
            dk_ref[:, sl] = gk.astype(dk_ref.dtype)
            dv_ref[:, sl] = gv.astype(dv_ref.dtype)
            dz_ref[:, sl] = gz.astype(dz_ref.dtype)
            dlg = dlg + glg
            dpar = [dpar[0] + ga, dpar[1] + gdt, dpar[2] + gnw]
        dlg_ref[...] = dlg
        first = jnp.logical_and(jnp.logical_and(b == 0, grp == 0), n == 0)
        for ref, gv_ in zip((da_ref, ddt_ref, dnw_ref), dpar):
            @pl.when(first)
            def _(ref=ref, gv_=gv_):
                ref[...] = gv_

            @pl.when(jnp.logical_not(first))
            def _(ref=ref, gv_=gv_):
                ref[...] += gv_

    rowb = pl.BlockSpec((CHUNK, W), lambda b, g, n: (b * NC + nb(n), g))
    return pl.pallas_call(
        body, name=name, grid=(Bl, G, NC),
        in_specs=[*qkv, zs, lgs, pv, pv, pv,
                  pl.BlockSpec((None, HP, None, HEAD_DIM, HEAD_DIM), lambda b, g, n: (b, g, nb(n), 0, 0)),
                  pl.BlockSpec((CHUNK, W), lambda b, g, n: (b * NC + nb(n), dyoff // W + g))],
        out_specs=[rowb, rowb, rowb, rowb,
                   pl.BlockSpec((None, CHUNK, LANE), lambda b, g, n: (g, b * NC + nb(n), 0)),
                   pv, pv, pv],
        out_shape=[jax.ShapeDtypeStruct((T, DN), F32)] * 3 + [jax.ShapeDtypeStruct((T, DN), BF16),
                   jax.ShapeDtypeStruct((G, T, LANE), F32)] + [jax.ShapeDtypeStruct((1, LANE), F32)] * 3,
        scratch_shapes=[pltpu.VMEM((HP, HEAD_DIM, HEAD_DIM), F32)],
        compiler_params=_cparams("arbitrary", "arbitrary", "arbitrary"),
    )(qkv_c, qkv_c, qkv_c, proj, proj, a_vec, dt_vec, nw, ssave, dy)


def _place():
    x, y, c = lax.axis_index("x"), lax.axis_index("y"), lax.axis_index("c")
    chips = [(1 - x, y), (x, 1 - y), (1 - x, 1 - y)]
    return x, y, c, chips


def _all_gather(shards, name):
    n = len(shards)

    def body(*refs):
        ins, outs = refs[:n], refs[n:2 * n]
        send_sems, recv_sems, local_sems = refs[2 * n:]
        x, y, c, chips = _place()
        me, sibling = (x, y, c), (x, y, 1 - c)

        def copy(a, k, block, to, src=None):
            dst = outs[a].at[4 * block[0] + 2 * block[1] + block[2]]
            return pltpu.make_async_remote_copy(
                src_ref=dst if src is None else src, dst_ref=dst,
                send_sem=send_sems.at[a * 7 + k], recv_sem=recv_sems.at[a * 7 + k],
                device_id=to, device_id_type=MESH)

        mine = [pltpu.make_async_copy(ins[a], outs[a].at[4 * x + 2 * y + c], local_sems.at[a]) for a in range(n)]
        for cp in mine:
            cp.start()
        first = []
        for a in range(n):
            first.append(copy(a, 0, me, sibling, src=ins[a]))
            first += [copy(a, 1 + j, me, (*chip, c), src=ins[a]) for j, chip in enumerate(chips)]
        for cp in first:
            cp.start()
        passed = []
        for j, chip in enumerate(chips):
            for a in range(n):
                copy(a, 1 + j, (*chip, c), me).wait_recv()
                fwd = copy(a, 4 + j, (*chip, c), sibling)
                fwd.start()
                passed.append(fwd)
        for a in range(n):
            copy(a, 0, sibling, me).wait_recv()
            for j, chip in enumerate(chips):
                copy(a, 4 + j, (*chip, 1 - c), me).wait_recv()
        for cp in first + passed:
            cp.wait_send()
        for cp in mine:
            cp.wait()

    return pl.pallas_call(
        body, name=name,
        in_specs=[_any_spec()] * n, out_specs=[_any_spec()] * n,
        out_shape=[jax.ShapeDtypeStruct((N_DEV,) + s.shape, s.dtype) for s in shards],
        scratch_shapes=[pltpu.SemaphoreType.DMA((7 * n,)), pltpu.SemaphoreType.DMA((7 * n,)),
                        pltpu.SemaphoreType.DMA((n,))],
    )(*shards)


def _sibling_exchange(grads, name):
    n = len(grads)

    def body(*refs):
        ins, outs = refs[:n], refs[n:2 * n]
        send_sems, recv_sems = refs[2 * n:]
        x, y, c, _ = _place()
        cps = [pltpu.make_async_remote_copy(
            src_ref=ins[a].at[:, 1 - c], dst_ref=outs[a],
            send_sem=send_sems.at[a], recv_sem=recv_sems.at[a],
            device_id=(x, y, 1 - c), device_id_type=MESH) for a in range(n)]
        for cp in cps:
            cp.start()
        for cp in cps:
            cp.wait()

    return pl.pallas_call(
        body, name=name,
        in_specs=[_any_spec()] * n, out_specs=[_any_spec()] * n,
        out_shape=[jax.ShapeDtypeStruct((N_CHIP,) + g.shape[2:], g.dtype) for g in grads],
        scratch_shapes=[pltpu.SemaphoreType.DMA((n,)), pltpu.SemaphoreType.DMA((n,))],
    )(*grads)


def _chip_exchange(parts, lands, layer, name):
    n = len(parts)

    def body(*refs):
        ins, outs = refs[:n], refs[2 * n:3 * n]
        send_sems, recv_sems, local_sems = refs[3 * n:]
        x, y, c, chips = _place()
        myq = 2 * x + y
        mine = [pltpu.make_async_copy(ins[a].at[myq], outs[a].at[layer, myq], local_sems.at[a]) for a in range(n)]
        for cp in mine:
            cp.start()
        sends = []
        for a in range(n):
            for j, (cx, cy) in enumerate(chips):
                sends.append(pltpu.make_async_remote_copy(
                    src_ref=ins[a].at[2 * cx + cy], dst_ref=outs[a].at[layer, myq],
                    send_sem=send_sems.at[a * 3 + j], recv_sem=recv_sems.at[a * 3 + j],
                    device_id=(cx, cy, c), device_id_type=MESH))
        for cp in sends:
            cp.start()
        for a in range(n):
            for j, (cx, cy) in enumerate(chips):
                pltpu.make_async_remote_copy(
                    src_ref=ins[a].at[2 * cx + cy], dst_ref=outs[a].at[layer, 2 * cx + cy],
                    send_sem=send_sems.at[a * 3 + j], recv_sem=recv_sems.at[a * 3 + j],
                    device_id=(cx, cy, c), device_id_type=MESH).wait_recv()
        for cp in sends:
            cp.wait_send()
        for cp in mine:
            cp.wait()

    return pl.pallas_call(
        body, name=name,
        in_specs=[_any_spec()] * (2 * n), out_specs=[_any_spec()] * n,
        out_shape=[jax.ShapeDtypeStruct(l.shape, l.dtype) for l in lands],
        input_output_aliases={n + a: a for a in range(n)},
        scratch_shapes=[pltpu.SemaphoreType.DMA((3 * n,)), pltpu.SemaphoreType.DMA((3 * n,)),
                        pltpu.SemaphoreType.DMA((n,))],
    )(*parts, *lands)


def _pair_sum(g, land, name, tr=256):
    _, _, R, C = g.shape
    tr = _tile(R, tr, SUBLANE)
    core = lax.axis_index("c").astype(jnp.int32).reshape(1)

    def body(c_ref, g_ref, l_ref, o_ref):
        o_ref[...] = (g_ref[...] + l_ref[...]).astype(o_ref.dtype)

    return pl.pallas_call(
        body, name=name,
        grid_spec=pltpu.PrefetchScalarGridSpec(
            num_scalar_prefetch=1, grid=(N_CHIP, R // tr),
            in_specs=[pl.BlockSpec((None, None, tr, C), lambda q, i, cr: (q, cr[0], i, 0)),
                      pl.BlockSpec((None, tr, C), lambda q, i, cr: (q, i, 0))],
            out_specs=pl.BlockSpec((None, tr, C), lambda q, i, cr: (q, i, 0))),
        out_shape=jax.ShapeDtypeStruct((N_CHIP, R, C), BF16),
        compiler_params=_cparams("parallel", "parallel"),
    )(core, g, land)


def _adam_math(w, g, m, v):
    m = ADAM_B1 * m + (1.0 - ADAM_B1) * g
    v = ADAM_B2 * v + (1.0 - ADAM_B2) * jnp.square(g)
    m_hat = m / (1.0 - ADAM_B1 ** ADAM_STEP)
    v_hat = v / (1.0 - ADAM_B2 ** ADAM_STEP)
    delta = -ADAM_LR * (m_hat / (jnp.sqrt(v_hat) + ADAM_EPS) + ADAM_WD * w)
    return delta, m, v


def _adam_landed(land, w, m, v, name, tr=256):
    L, _, R, C = land.shape
    tr = _tile(R, tr, SUBLANE)

    def body(l_ref, w_ref, m_ref, v_ref, g_ref, d_ref, nm_ref, nv_ref):
        g = l_ref[0].astype(F32)
        for q in range(1, N_CHIP):
            g = g + l_ref[q].astype(F32)
        d, nm, nv = _adam_math(w_ref[...], g, m_ref[...], v_ref[...])
        g_ref[...], d_ref[...], nm_ref[...], nv_ref[...] = g, d, nm, nv

    blk = pl.BlockSpec((None, tr, C), lambda l, i: (l, i, 0))
    return pl.pallas_call(
        body, name=name, grid=(L, R // tr),
        in_specs=[pl.BlockSpec((None, N_CHIP, tr, C), lambda l, i: (l, 0, i, 0)), blk, blk, blk],
        out_specs=[blk] * 4, out_shape=[jax.ShapeDtypeStruct((L, R, C), F32)] * 4,
        compiler_params=_cparams("parallel", "parallel"),
    )(land, w, m, v)


def _adam(g, w, m, v, name, tr=256):
    R, C = w.shape
    tr = _tile(R, tr, SUBLANE)

    def body(g_ref, w_ref, m_ref, v_ref, d_ref, nm_ref, nv_ref):
        d_ref[...], nm_ref[...], nv_ref[...] = _adam_math(w_ref[...], g_ref[...], m_ref[...], v_ref[...])

    blk = pl.BlockSpec((tr, C), lambda i: (i, 0))
    return pl.pallas_call(
        body, name=name, grid=(R // tr,),
        in_specs=[blk] * 4, out_specs=[blk] * 3, out_shape=[jax.ShapeDtypeStruct((R, C), F32)] * 3,
        compiler_params=_cparams("parallel"),
    )(g, w, m, v)


def _bias_sum(a, name, tn=2048):
    B, N = a.shape
    tn = _tile(N, tn)

    def body(a_ref, o_ref):
        o_ref[...] = jnp.sum(a_ref[...], axis=0, keepdims=True)

    return pl.pallas_call(
        body, name=name, grid=(N // tn,),
        in_specs=[pl.BlockSpec((B, tn), lambda j: (0, j))],
        out_specs=pl.BlockSpec((1, tn), lambda j: (0, j)),
        out_shape=jax.ShapeDtypeStruct((1, N), F32),
        compiler_params=_cparams("parallel"),
    )(a)


def _sum_devices(stack, name):
    _, R, C = stack.shape

    def body(s_ref, o_ref):
        acc = s_ref[0]
        for d in range(1, N_DEV):
            acc = acc + s_ref[d]
        o_ref[...] = acc

    return pl.pallas_call(
        body, name=name, grid=(1,),
        in_specs=[pl.BlockSpec((N_DEV, R, C), lambda i: (0, 0, 0))],
        out_specs=pl.BlockSpec((R, C), lambda i: (0, 0)),
        out_shape=jax.ShapeDtypeStruct((R, C), F32),
    )(stack)


def _pack(arrs):
    flat = jnp.concatenate([a.reshape(-1) for a in arrs])
    n = flat.shape[0]
    rows = -(-n // (LANE * SUBLANE)) * SUBLANE
    flat = jnp.pad(flat, (0, rows * LANE - n))
    return flat.reshape(rows, LANE)


def _unpack(packed, shapes):
    flat = packed.reshape(-1)
    out, pos = [], 0
    for s in shapes:
        n = int(np.prod(s))
        out.append(flat[pos:pos + n].reshape(s))
        pos += n
    return out


def _cols_to_shards(g, n_cols):
    R = g.shape[0]
    return g[:, :n_cols].reshape(R, N_DEV, n_cols // N_DEV).transpose(1, 0, 2).reshape(N_CHIP, 2, R, n_cols // N_DEV)


def _rows_to_shards(g):
    R, C = g.shape
    return g.reshape(N_CHIP, 2, R // N_DEV, C)


def _shards_to_cols(w8, pad_to=None):
    _, R, Cs = w8.shape
    w = w8.transpose(1, 0, 2).reshape(R, N_DEV * Cs)
    if pad_to is not None and pad_to > N_DEV * Cs:
        w = jnp.pad(w, ((0, 0), (0, pad_to - N_DEV * Cs)))
    return w


def kernel(x, c, w_ada, b_ada, norm1_w, w_in, conv_dw_w, conv_dw_b, conv_ln_w, conv_ln_b, w_pw2, conv_out_norm_w, qkv_conv_w, a_log, dt_bias, dn_norm_w, w_out, norm2_w, w_up, w_down, final_ada_w, final_ada_b, final_norm_w, loss_target, m_w_ada, m_b_ada, m_norm1_w, m_w_in, m_conv_dw_w, m_conv_dw_b, m_conv_ln_w, m_conv_ln_b, m_w_pw2, m_conv_out_norm_w, m_qkv_conv_w, m_a_log, m_dt_bias, m_dn_norm_w, m_w_out, m_norm2_w, m_w_up, m_w_down, m_final_ada_w, m_final_ada_b, m_final_norm_w, v_w_ada, v_b_ada, v_norm1_w, v_w_in, v_conv_dw_w, v_conv_dw_b, v_conv_ln_w, v_conv_ln_b, v_w_pw2, v_conv_out_norm_w, v_qkv_conv_w, v_a_log, v_dt_bias, v_dn_norm_w, v_w_out, v_norm2_w, v_w_up, v_w_down, v_final_ada_w, v_final_ada_b, v_final_norm_w):
    Bl, S, D = x.shape
    T = Bl * S
    L = w_ada.shape[0]
    CC = conv_ln_w.shape[1]
    DN = qkv_conv_w.shape[2] * N_DEV // 3
    H = a_log.shape[1]
    IN_COLS = w_in.shape[2] * N_DEV
    IN_PAD = -(-IN_COLS // LANE) * LANE
    DFF = w_up.shape[2] * N_DEV
    NMOD = w_ada.shape[2] * N_DEV // D
    ada_cols = w_ada.shape[2]
    fada_cols = final_ada_w.shape[1]
    HP = min(4, H)
    assert DN == H * HEAD_DIM and NMOD == 6 and S % CHUNK == 0

    dev = 4 * lax.axis_index("x") + 2 * lax.axis_index("y") + lax.axis_index("c")
    xf = x.reshape(T, D)
    tgt = loss_target.reshape(T, D)

    c8, cdw8, qcw8 = _all_gather([c, conv_dw_w, qkv_conv_w], "ag_small")
    c_all = c8.reshape(N_DEV * Bl, D)
    conv_w_full = cdw8.transpose(1, 2, 0, 3).reshape(L, conv_dw_w.shape[1], CC)
    qkv_w_full = qcw8.transpose(1, 2, 0, 3).reshape(L, qkv_conv_w.shape[1], 3 * DN)
    (c_act,) = _rowwise(lambda cv: (_silu(cv),), [(c_all, 0, D)], [], [], [(D, F32)], "c_act", tile=N_DEV * Bl)
    mod_cols = []
    for l in range(L):
        bias = lax.dynamic_slice_in_dim(b_ada[l], dev * ada_cols, ada_cols).reshape(1, ada_cols)
        mod_cols.append(_matmul(c_act, w_ada, "nn", f"mod{l}", bias=bias, b_layer=l, tk=2048))
    bias_f = lax.dynamic_slice_in_dim(final_ada_b, dev * fada_cols, fada_cols).reshape(1, fada_cols)
    mod_cols.append(_matmul(c_act, final_ada_w, "nn", "modf", bias=bias_f, tk=2048))
    (mod8,) = _all_gather([jnp.concatenate(mod_cols, axis=1)], "ag_mod")
    mod8 = lax.dynamic_slice_in_dim(mod8, dev * Bl, Bl, axis=1)
    mod_l = mod8[:, :, :L * ada_cols].reshape(N_DEV, Bl, L, ada_cols).transpose(1, 2, 0, 3).reshape(Bl, L, NMOD, 1, D)
    mod_f = mod8[:, :, L * ada_cols:].transpose(1, 0, 2).reshape(Bl, 2, 1, D)
    shift_f, scale_f = mod_f[:, 0], mod_f[:, 1]

    def modv(l, k):
        return mod_l[:, l, k]

    pv = lambda a: a.reshape(1, -1)
    padl = lambda a: jnp.pad(a, (0, LANE - a.shape[0])).reshape(1, LANE)

    saved = []
    wts = []
    x1p = mp = None
    for l in range(L):
        w_in8, w_pw28, w_out8, w_up8, w_down8 = _all_gather(
            [w_in[l].astype(BF16), w_pw2[l].astype(BF16), w_out[l].astype(BF16), w_up[l].astype(BF16),
             w_down[l].astype(BF16)], f"ag_w{l}")
        W = dict(w_in=_shards_to_cols(w_in8, IN_PAD), w_pw2=w_pw28.reshape(CC, CC), w_out=w_out8.reshape(D, D),
                 w_up=_shards_to_cols(w_up8), w_down=w_down8.reshape(DFF, D))
        wts.append(W)
        sv = {}
        if l == 0:
            (h,) = _rowwise(lambda *v: (_f_first(*v)[1],), [(xf, 0, D)], [modv(l, 0), modv(l, 1)], [pv(norm1_w[l])],
                            [(D, BF16)], f"pre{l}")
            xl = xf
        else:
            xl, h = _rowwise(_f_join, [(x1p, 0, D), (mp, 0, D)], [modv(l - 1, 5), modv(l, 0), modv(l, 1)],
                             [pv(norm1_w[l])], [(D, F32), (D, BF16)], f"pre{l}")
        proj = _matmul(h, W["w_in"], "nn", f"proj{l}")
        (u0,) = _rowwise(_f_glu, [(proj, 0, CC), (proj, 1, CC)], [], [], [(CC, F32)], f"glu{l}")
        u1 = _dwconv_fwd(u0, 0, conv_w_full[l], 0, CC, S, f"conv{l}")
        cpv = [pv(conv_dw_b[l]), pv(conv_ln_w[l]), pv(conv_ln_b[l])]
        (u2,) = _rowwise(_f_convmid, [(u1, 0, CC)], [], cpv, [(CC, BF16)], f"convmid{l}")
        u3 = _matmul(u2, W["w_pw2"], "nn", f"pw2{l}")
        (y_conv,) = _rowwise(_f_rms, [(u3, 0, CC)], [], [pv(conv_out_norm_w[l])], [(CC, BF16)], f"convout{l}")
        qkv_c = _dwconv_fwd(proj, 2 * CC, qkv_w_full[l], 0, 3 * DN, S, f"qkvconv{l}")
        dvec = [padl(a_log[l]), padl(dt_bias[l]), pv(dn_norm_w[l])]
        y_dn, ssave = _delta_fwd(qkv_c, proj, *dvec, DN, CC, S, f"delta{l}", HP)
        ycat = jnp.concatenate([y_conv, y_dn], axis=1)
        y = _matmul(ycat, W["w_out"], "nn", f"out{l}")
        x1, h2 = _rowwise(_f_mid, [(xl, 0, D), (y, 0, D)], [modv(l, 2), modv(l, 3), modv(l, 4)], [pv(norm2_w[l])],
                          [(D, F32), (D, BF16)], f"mid{l}")
        a_act, r_act = _matmul(h2, W["w_up"], "nn", f"up{l}", out_dtypes=(BF16, BF16),
                               epi=lambda acc: (acc, jnp.square(jnp.maximum(acc, 0.0))))
        m = _matmul(r_act, W["w_down"], "nn", f"down{l}")
        sv.update(xl=xl, h=h, proj=proj, u0=u0, u1=u1, u2=u2, u3=u3, qkv_c=qkv_c, ssave=ssave, ycat=ycat, y=y,
                  x1=x1, h2=h2, a=a_act, r=r_act, m=m, x1p=x1p, mp=mp, cpv=cpv, dvec=dvec)
        saved.append(sv)
        x1p, mp = x1, m

    dx1, dm, dgate2, dshift_f, dscale_f, dwf, loss_part = _final_loss(
        x1p, mp, tgt, modv(L - 1, 5), shift_f, scale_f, pv(final_norm_w), "final")

    big = ("w_in", "w_pw2", "w_out", "w_up", "w_down")
    wstack = dict(w_in=(w_in, m_w_in, v_w_in), w_pw2=(w_pw2, m_w_pw2, v_w_pw2), w_out=(w_out, m_w_out, v_w_out),
                  w_up=(w_up, m_w_up, v_w_up), w_down=(w_down, m_w_down, v_w_down))
    lands = [jnp.zeros((L, N_CHIP) + wstack[k][0].shape[1:], BF16) for k in big]
    dmod = [[None] * NMOD for _ in range(L)]
    small = {k: [None] * L for k in ("norm1_w", "conv_dw_w", "conv_dw_b", "conv_ln_w", "conv_ln_b", "conv_out_norm_w",
                                     "qkv_conv_w", "a_log", "dt_bias", "dn_norm_w", "norm2_w")}
    for l in reversed(range(L)):
        sv, W = saved[l], wts[l]
        dmod[l][5] = dgate2
        da = _matmul(dm, W["w_down"], "nt", f"d_r{l}", out_dtypes=(BF16,), sides=(sv["a"],),
                     epi=lambda acc, a: (acc * (2.0 * jnp.maximum(a.astype(F32), 0.0)),))
        g_down = _matmul(sv["r"], dm, "tn", f"g_down{l}")
        dh2 = _matmul(da, W["w_up"], "nt", f"d_h2{l}")
        g_up = _matmul(sv["h2"], da, "tn", f"g_up{l}")
        (dxl, dy), (dmod[l][2], dmod[l][3], dmod[l][4]), (small["norm2_w"][l],) = _rowwise_bwd(
            _f_mid, [(sv["xl"], 0, D), (sv["y"], 0, D)], [modv(l, 2), modv(l, 3), modv(l, 4)], [pv(norm2_w[l])],
            [(dx1, 0, D), (dh2, 0, D)], [F32, BF16], f"mid_b{l}")
        dycat = _matmul(dy, W["w_out"], "nt", f"d_ycat{l}")
        g_out = _matmul(sv["ycat"], dy, "tn", f"g_out{l}")
        (du3,), _, (small["conv_out_norm_w"][l],) = _rowwise_bwd(
            _f_rms, [(sv["u3"], 0, CC)], [], [pv(conv_out_norm_w[l])], [(dycat, 0, CC)], [BF16], f"convout_b{l}", tile=256)
        du2 = _matmul(du3, W["w_pw2"], "nt", f"d_u2{l}")
        g_pw2 = _matmul(sv["u2"], du3, "tn", f"g_pw2{l}")
        (du1,), _, (small["conv_dw_b"][l], small["conv_ln_w"][l], small["conv_ln_b"][l]) = _rowwise_bwd(
            _f_convmid, [(sv["u1"], 0, CC)], [], sv["cpv"], [(du2, 0, CC)], [F32], f"convmid_b{l}", tile=256)
        du0, small["conv_dw_w"][l] = _dwconv_bwd(sv["u0"], 0, conv_w_full[l], 0, du1, CC, S, f"conv_b{l}", F32)
        (dval, dgate), _, _ = _rowwise_bwd(_f_glu, [(sv["proj"], 0, CC), (sv["proj"], 1, CC)], [], [],
                                           [(du0, 0, CC)], [BF16, BF16], f"glu_b{l}", tile=256)
        dq_c, dk_c, dv_c, dz, dlg, g_alog, g_dt, small["dn_norm_w"][l] = _delta_bwd(
            sv["qkv_c"], sv["proj"], *sv["dvec"], sv["ssave"], dycat, CC, DN, CC, S, f"delta_b{l}", HP)
        small["a_log"][l], small["dt_bias"][l] = g_alog[0, :H], g_dt[0, :H]
        dqkv, gqw = [], []
        for p, dpc in enumerate((dq_c, dk_c, dv_c)):
            dpart, gw = _dwconv_bwd(sv["proj"], 2 * CC + p * DN, qkv_w_full[l], p * DN, dpc, DN, S,
                                    f"qkvconv_b{l}_{p}", BF16)
            dqkv.append(dpart)
            gqw.append(gw)
        small["qkv_conv_w"][l] = jnp.concatenate(gqw, axis=1)
        (dlog,) = _rowwise(lambda *gs: (sum(gs[1:], gs[0]),), [(dlg[g], 0, LANE) for g in range(dlg.shape[0])], [], [],
                           [(LANE, BF16)], f"dlog{l}")
        pieces = [dval, dgate, *dqkv, dz, dlog]
        dproj = jnp.concatenate(pieces, axis=1)
        dh = _matmul(dproj, W["w_in"], "nt", f"d_h{l}")
        g_in = _matmul(sv["h"], dproj, "tn", f"g_in{l}")
        if l == 0:
            (grad_x,), (dmod[l][0], dmod[l][1]), (small["norm1_w"][l],) = _rowwise_bwd(
                _f_first, [(xf, 0, D)], [modv(l, 0), modv(l, 1)], [pv(norm1_w[l])],
                [(dxl, 0, D), (dh, 0, D)], [F32], f"pre_b{l}")
        else:
            (dx1, dm), (dgate2, dmod[l][0], dmod[l][1]), (small["norm1_w"][l],) = _rowwise_bwd(
                _f_join, [(sv["x1p"], 0, D), (sv["mp"], 0, D)], [modv(l - 1, 5), modv(l, 0), modv(l, 1)],
                [pv(norm1_w[l])], [(dxl, 0, D), (dh, 0, D)], [F32, BF16], f"pre_b{l}")
        gl = [_cols_to_shards(g_in, IN_COLS), _rows_to_shards(g_pw2), _rows_to_shards(g_out),
              _cols_to_shards(g_up, DFF), _rows_to_shards(g_down)]
        sib = _sibling_exchange(gl, f"rs_sib{l}")
        parts = [_pair_sum(g, s, f"rs_sum{l}_{k}") for k, (g, s) in enumerate(zip(gl, sib))]
        lands = _chip_exchange(parts, lands, l, f"rs_chip{l}")

    small_list = [jnp.stack(small[k]) for k in ("norm1_w", "conv_dw_w", "conv_dw_b", "conv_ln_w", "conv_ln_b",
                                                "conv_out_norm_w", "qkv_conv_w", "a_log", "dt_bias", "dn_norm_w",
                                                "norm2_w")]
    small_list += [dwf.reshape(-1), loss_part[0, :1]]
    small_shapes = [a.shape for a in small_list]
    dmod_local = jnp.concatenate(
        [jnp.concatenate([dmod[l][k].reshape(Bl, D) for k in range(NMOD)], axis=1) for l in range(L)]
        + [dshift_f.reshape(Bl, D), dscale_f.reshape(Bl, D)], axis=1)
    packed8, dmod8 = _all_gather([_pack(small_list), dmod_local], "ag_grads")
    summed = _unpack(_sum_devices(packed8, "sum_small"), small_shapes)
    (g_norm1, g_cdw_full, g_cdb, g_clw, g_clb, g_con, g_qcw_full, g_alog, g_dtb, g_dnw, g_norm2, g_fnw, loss) = summed
    loss = loss.reshape(())
    g_cdw = lax.dynamic_slice_in_dim(g_cdw_full, dev * conv_dw_w.shape[2], conv_dw_w.shape[2], axis=2)
    g_qcw = lax.dynamic_slice_in_dim(g_qcw_full, dev * qkv_conv_w.shape[2], qkv_conv_w.shape[2], axis=2)

    dmod_all = dmod8.reshape(N_DEV * Bl, L * NMOD * D + 2 * D)
    g_w_ada, g_b_ada = [], []
    for l in range(L):
        dm_l = dmod_all[:, l * NMOD * D:(l + 1) * NMOD * D]
        cols = lax.dynamic_slice_in_dim(dm_l, dev * ada_cols, ada_cols, axis=1)
        g_w_ada.append(_matmul(c_act, cols, "tn", f"g_ada{l}", precision=HI, tk=N_DEV * Bl))
    dm_f = dmod_all[:, L * NMOD * D:]
    cols_f = lax.dynamic_slice_in_dim(dm_f, dev * fada_cols, fada_cols, axis=1)
    g_fada_w = _matmul(c_act, cols_f, "tn", "g_fada", precision=HI, tk=N_DEV * Bl)
    g_w_ada = jnp.stack(g_w_ada)
    bsum = _bias_sum(dmod_all, "g_bias")
    g_b_ada = bsum[0, :L * NMOD * D].reshape(L, NMOD * D)
    g_fada_b = bsum[0, L * NMOD * D:]

    big_out = {}
    for k, land in zip(big, lands):
        w_, m_, v_ = wstack[k]
        big_out[k] = _adam_landed(land, w_, m_, v_, f"adam_{k}")
    R_ada = L * w_ada.shape[1]
    ada = _adam(g_w_ada.reshape(R_ada, ada_cols), w_ada.reshape(R_ada, ada_cols), m_w_ada.reshape(R_ada, ada_cols),
                v_w_ada.reshape(R_ada, ada_cols), "adam_w_ada")
    ada = [a.reshape(w_ada.shape) for a in ada]
    fada = _adam(g_fada_w, final_ada_w, m_final_ada_w, v_final_ada_w, "adam_fada")
    sm_names = ["b_ada", "norm1_w", "conv_dw_w", "conv_dw_b", "conv_ln_w", "conv_ln_b", "conv_out_norm_w", "qkv_conv_w",
                "a_log", "dt_bias", "dn_norm_w", "norm2_w", "final_ada_b", "final_norm_w"]
    sm_g = dict(b_ada=g_b_ada, norm1_w=g_norm1, conv_dw_w=g_cdw, conv_dw_b=g_cdb, conv_ln_w=g_clw, conv_ln_b=g_clb,
                conv_out_norm_w=g_con, qkv_conv_w=g_qcw, a_log=g_alog, dt_bias=g_dtb, dn_norm_w=g_dnw, norm2_w=g_norm2,
                final_ada_b=g_fada_b, final_norm_w=g_fnw)
    sm_w = dict(b_ada=(b_ada, m_b_ada, v_b_ada), norm1_w=(norm1_w, m_norm1_w, v_norm1_w),
                conv_dw_w=(conv_dw_w, m_conv_dw_w, v_conv_dw_w), conv_dw_b=(conv_dw_b, m_conv_dw_b, v_conv_dw_b),
                conv_ln_w=(conv_ln_w, m_conv_ln_w, v_conv_ln_w), conv_ln_b=(conv_ln_b, m_conv_ln_b, v_conv_ln_b),
                conv_out_norm_w=(conv_out_norm_w, m_conv_out_norm_w, v_conv_out_norm_w),
                qkv_conv_w=(qkv_conv_w, m_qkv_conv_w, v_qkv_conv_w), a_log=(a_log, m_a_log, v_a_log),
                dt_bias=(dt_bias, m_dt_bias, v_dt_bias), dn_norm_w=(dn_norm_w, m_dn_norm_w, v_dn_norm_w),
                norm2_w=(norm2_w, m_norm2_w, v_norm2_w), final_ada_b=(final_ada_b, m_final_ada_b, v_final_ada_b),
                final_norm_w=(final_norm_w, m_final_norm_w, v_final_norm_w))
    sm_shapes = [sm_w[k][0].shape for k in sm_names]
    sm = _adam(_pack([sm_g[k] for k in sm_names]), _pack([sm_w[k][0] for k in sm_names]),
               _pack([sm_w[k][1] for k in sm_names]), _pack([sm_w[k][2] for k in sm_names]), "adam_small")
    sm_d, sm_m, sm_v = (dict(zip(sm_names, _unpack(a, sm_shapes))) for a in sm)
    for k in sm_names:
        sm_g[k] = sm_g[k].reshape(sm_w[k][0].shape)

    grads = dict(w_ada=g_w_ada, final_ada_w=g_fada_w, **{k: big_out[k][0] for k in big}, **sm_g)
    deltas = dict(w_ada=ada[0], final_ada_w=fada[0], **{k: big_out[k][1] for k in big}, **sm_d)
    new_m = dict(w_ada=ada[1], final_ada_w=fada[1], **{k: big_out[k][2] for k in big}, **sm_m)
    new_v = dict(w_ada=ada[2], final_ada_w=fada[2], **{k: big_out[k][3] for k in big}, **sm_v)
    order = ["w_ada", "b_ada", "norm1_w", "w_in", "conv_dw_w", "conv_dw_b", "conv_ln_w", "conv_ln_b", "w_pw2",
             "conv_out_norm_w", "qkv_conv_w", "a_log", "dt_bias", "dn_norm_w", "w_out", "norm2_w", "w_up", "w_down",
             "final_ada_w", "final_ada_b", "final_norm_w"]
    return (loss, grad_x.reshape(Bl, S, D), *[grads[k] for k in order], *[deltas[k] for k in order],
            *[new_m[k] for k in order], *[new_v[k] for k in order])
```

```python
import functools

import numpy as np
import jax
import jax.numpy as jnp
from jax import lax
from jax.experimental import pallas as pl
from jax.experimental.pallas import tpu as pltpu

F32, BF16 = jnp.float32, jnp.bfloat16
EPS = 1e-6
CHUNK = 64
HEAD_DIM = 128
LANE = 128
SUBLANE = 8
N_DEV = 8
N_CHIP = 4
VMEM_LIMIT = 48 * 1024 * 1024
ADAM_LR, ADAM_B1, ADAM_B2, ADAM_EPS, ADAM_WD, ADAM_STEP = 0.001, 0.9, 0.999, 1e-08, 0.01, 10
MESH = pl.DeviceIdType.MESH
HI = lax.Precision.HIGHEST
NN, NT, TN = ((1,), (0,)), ((1,), (1,)), ((0,), (0,))


def _any_spec():
    return pl.BlockSpec(memory_space=pl.ANY)


def _cparams(*sem):
    return pltpu.CompilerParams(dimension_semantics=sem or None, vmem_limit_bytes=VMEM_LIMIT)


def _tile(dim, target, unit=LANE):
    best, t = None, unit
    while t <= min(dim, target):
        if dim % t == 0:
            best = t
        t += unit
    return best or dim


def _sigmoid(x):
    return jax.nn.sigmoid(x)


def _silu(x):
    return x * jax.nn.sigmoid(x)


def _softplus(x):
    return jnp.maximum(x, 0.0) + jnp.log(1.0 + jnp.exp(-jnp.abs(x)))


def _rms(x, w):
    return x * lax.rsqrt(jnp.mean(x * x, axis=-1, keepdims=True) + EPS) * w


def _matmul(a, b, mode, name, out_dtypes=(F32,), epi=None, sides=(), bias=None, precision=None,
            b_layer=None, tm=1024, tn=1024, tk=1024, comm=None):
    bshape = b.shape[1:] if b_layer is not None else b.shape
    if mode == "nn":
        (M, K), (K2, N) = a.shape, bshape
    elif mode == "nt":
        (M, K), (N, K2) = a.shape, bshape
    else:
        (K, M), (K2, N) = a.shape, bshape
    assert K == K2, (a.shape, b.shape, mode)
    tm, tn, tk = _tile(M, tm), _tile(N, tn), _tile(K, tk)
    nk = K // tk
    dims = {"nn": NN, "nt": NT, "tn": TN}[mode]
    n_side, n_out, has_bias = len(sides), len(out_dtypes), bias is not None

    if mode == "tn":
        a_spec = pl.BlockSpec((tk, tm), lambda i, j, k: (k, i))
    else:
        a_spec = pl.BlockSpec((tm, tk), lambda i, j, k: (i, k))
    if mode == "nt":
        bblk, bidx = (tn, tk), (lambda i, j, k: (j, k))
    else:
        bblk, bidx = (tk, tn), (lambda i, j, k: (k, j))
    if b_layer is not None:
        b_spec = pl.BlockSpec((None,) + bblk, lambda i, j, k: (b_layer,) + bidx(i, j, k))
    else:
        b_spec = pl.BlockSpec(bblk, bidx)
    mn_spec = pl.BlockSpec((tm, tn), lambda i, j, k: (i, j))
    in_specs = [a_spec, b_spec] + [mn_spec] * n_side
    if has_bias:
        in_specs.append(pl.BlockSpec((1, tn), lambda i, j, k: (0, j)))

    n_ci, n_co = (len(comm.ins), len(comm.out_shapes)) if comm is not None else (0, 0)
    ni, nj = M // tm, N // tn
    n_in = 2 + n_side + int(has_bias)

    def body(*refs):
        a_ref, b_ref = refs[:2]
        side_refs = refs[2:2 + n_side]
        bias_ref = refs[2 + n_side] if has_bias else None
        comm_in = refs[n_in:n_in + n_ci]
        pos = n_in + n_ci
        out_refs, comm_out = refs[pos:pos + n_out], refs[pos + n_out:pos + n_out + n_co]
        acc_ref, comm_sems = refs[pos + n_out + n_co], refs[pos + n_out + n_co + 1:]
        i, j, k = pl.program_id(0), pl.program_id(1), pl.program_id(2)

        if comm is not None:
            @pl.when(jnp.logical_and(jnp.logical_and(i == 0, j == 0), k == 0))
            def _():
                comm.start(comm_in, comm_out, comm_sems)

        @pl.when(k == 0)
        def _():
            acc_ref[...] = jnp.zeros_like(acc_ref)

        av, bv = a_ref[...], b_ref[...]
        if precision is None:
            av, bv = av.astype(BF16), bv.astype(BF16)
        acc_ref[...] += lax.dot_general(av, bv, (dims, ((), ())), precision=precision,
                                        preferred_element_type=F32)

        @pl.when(k == nk - 1)
        def _():
            acc = acc_ref[...]
            if has_bias:
                acc = acc + bias_ref[...]
            outs = (acc,) if epi is None else epi(acc, *[s[...] for s in side_refs])
            for r, o in zip(out_refs, outs):
                r[...] = o.astype(r.dtype)

        if comm is not None:
            @pl.when(jnp.logical_and(jnp.logical_and(i == ni - 1, j == nj - 1), k == nk - 1))
            def _():
                comm.finish(comm_in, comm_out, comm_sems)

    sem = ("arbitrary",) * 3 if comm is not None else ("parallel", "parallel", "arbitrary")
    res = pl.pallas_call(
        body, name=name, grid=(ni, nj, nk),
        in_specs=in_specs + [_any_spec()] * n_ci, out_specs=[mn_spec] * n_out + [_any_spec()] * n_co,
        out_shape=[jax.ShapeDtypeStruct((M, N), d) for d in out_dtypes] + (comm.out_shapes if comm is not None else []),
        input_output_aliases={n_in + ci: n_out + co for ci, co in comm.aliases.items()} if comm is not None else {},
        scratch_shapes=[pltpu.VMEM((tm, tn), F32)] + (comm.sems if comm is not None else []),
        compiler_params=_cparams(*sem),
    )(a, b, *sides, *([bias] if has_bias else []), *(comm.ins if comm is not None else []))
    main = res[0] if n_out == 1 else list(res[:n_out])
    return main if comm is None else (main, list(res[n_out:]))


def _row_specs(rows, bvecs, pvecs, tile, tpb):
    specs = [pl.BlockSpec((tile, w), lambda i, cb=cb: (i, cb)) for (_, cb, w) in rows]
    specs += [pl.BlockSpec((None, 1, v.shape[-1]), lambda i: (i // tpb, 0, 0)) for v in bvecs]
    specs += [pl.BlockSpec((1, v.shape[-1]), lambda i: (0, 0)) for v in pvecs]
    return specs


def _row_tiling(rows, bvecs, tile):
    T = rows[0][0].shape[0]
    Bl = bvecs[0].shape[0] if bvecs else 1
    tile = min(tile, T // Bl)
    assert (T // Bl) % tile == 0
    return T, tile, (T // Bl) // tile


def _rowwise(fn, rows, bvecs, pvecs, outs, name, tile=256):
    T, tile, tpb = _row_tiling(rows, bvecs, tile)
    n_in = len(rows) + len(bvecs) + len(pvecs)

    def body(*refs):
        vals = [r[...].astype(F32) for r in refs[:n_in]]
        res = fn(*vals)
        for r, o in zip(refs[n_in:], res):
            r[...] = o.astype(r.dtype)

    res = pl.pallas_call(
        body, name=name, grid=(T // tile,),
        in_specs=_row_specs(rows, bvecs, pvecs, tile, tpb),
        out_specs=[pl.BlockSpec((tile, w), lambda i: (i, 0)) for (w, _) in outs],
        out_shape=[jax.ShapeDtypeStruct((T, w), d) for (w, d) in outs],
        compiler_params=_cparams("parallel"),
    )(*[r[0] for r in rows], *bvecs, *pvecs)
    return res


def _rowwise_bwd(fn, rows, bvecs, pvecs, cots, row_grads, name, tile=128):
    T, tile, tpb = _row_tiling(rows, bvecs, tile)
    n_row, n_b, n_p, n_cot = len(rows), len(bvecs), len(pvecs), len(cots)
    n_in = n_row + n_b + n_p
    want = [k for k, d in enumerate(row_grads) if d is not None]

    def body(*refs):
        vals = [r[...].astype(F32) for r in refs[:n_in]]
        cot_vals = tuple(r[...].astype(F32) for r in refs[n_in:n_in + n_cot])
        out_refs = refs[n_in + n_cot:]
        _, vjp = jax.vjp(fn, *vals)
        g = vjp(cot_vals)
        i = pl.program_id(0)
        for r, k in zip(out_refs[:len(want)], want):
            r[...] = g[k].astype(r.dtype)
        pos = len(want)
        for q in range(n_b + n_p):
            ref, gv = out_refs[pos + q], g[n_row + q]
            first = (i % tpb == 0) if q < n_b else (i == 0)

            @pl.when(first)
            def _(ref=ref, gv=gv):
                ref[...] = gv

            @pl.when(jnp.logical_not(first))
            def _(ref=ref, gv=gv):
                ref[...] += gv

    out_specs = [pl.BlockSpec((tile, rows[k][2]), lambda i: (i, 0)) for k in want]
    out_specs += [pl.BlockSpec((None, 1, v.shape[-1]), lambda i: (i // tpb, 0, 0)) for v in bvecs]
    out_specs += [pl.BlockSpec((1, v.shape[-1]), lambda i: (0, 0)) for v in pvecs]
    out_shape = [jax.ShapeDtypeStruct((T, rows[k][2]), row_grads[k]) for k in want]
    out_shape += [jax.ShapeDtypeStruct(v.shape, F32) for v in bvecs]
    out_shape += [jax.ShapeDtypeStruct(v.shape, F32) for v in pvecs]
    res = pl.pallas_call(
        body, name=name, grid=(T // tile,),
        in_specs=_row_specs(rows, bvecs, pvecs, tile, tpb) + _row_specs(cots, [], [], tile, tpb),
        out_specs=out_specs, out_shape=out_shape,
        compiler_params=_cparams("arbitrary"),
    )(*[r[0] for r in rows], *bvecs, *pvecs, *[c[0] for c in cots])
    nw = len(want)
    return list(res[:nw]), list(res[nw:nw + n_b]), list(res[nw + n_b:])


def _f_first(x, shift, scale, w):
    return x, _rms(x, w) * (1.0 + scale) + shift


def _f_join(x1p, mp, gate2p, shift, scale, w):
    x = x1p + gate2p * mp
    return x, _rms(x, w) * (1.0 + scale) + shift


def _f_mid(x, y, gate1, shift2, scale2, w2):
    x1 = x + gate1 * y
    return x1, _rms(x1, w2) * (1.0 + scale2) + shift2


def _f_glu(val, gate):
    return (val * _sigmoid(gate),)


def _f_convmid(u1, b, lnw, lnb):
    u = u1 + b
    mu = jnp.mean(u, axis=-1, keepdims=True)
    xc = u - mu
    y = xc * lax.rsqrt(jnp.mean(xc * xc, axis=-1, keepdims=True) + EPS) * lnw + lnb
    return (_silu(y),)


def _f_rms(u, w):
    return (_rms(u, w),)


def _final_loss(x1, m, tgt, gate2, shift, scale, wf, name, tile=128):
    T, D = x1.shape
    Bl = gate2.shape[0]
    tile = min(tile, T // Bl)
    tpb = (T // Bl) // tile

    def body(x1_ref, m_ref, t_ref, g_ref, sh_ref, sc_ref, w_ref,
             dx1_ref, dm_ref, dg_ref, dsh_ref, dsc_ref, dw_ref, loss_ref):
        tg = t_ref[...]

        def lossfn(x1v, mv, g2, sh, sc, w):
            x2 = x1v + g2 * mv
            y = _rms(x2, w) * (1.0 + sc) + sh
            err = y - tg
            per_row = jnp.mean(err * err, axis=-1, keepdims=True)
            return 0.5 * jnp.sum(per_row, axis=0, keepdims=True)

        val, vjp = jax.vjp(lossfn, x1_ref[...], m_ref[...], g_ref[...], sh_ref[...], sc_ref[...], w_ref[...])
        g = vjp(jnp.ones((1, 1), F32))
        dx1_ref[...] = g[0]
        dm_ref[...] = g[1].astype(dm_ref.dtype)
        i = pl.program_id(0)
        acc = [(dg_ref, g[2], i % tpb == 0), (dsh_ref, g[3], i % tpb == 0), (dsc_ref, g[4], i % tpb == 0),
               (dw_ref, g[5], i == 0), (loss_ref, jnp.broadcast_to(val, (1, LANE)), i == 0)]
        for ref, gv, first in acc:
            @pl.when(first)
            def _(ref=ref, gv=gv):
                ref[...] = gv

            @pl.when(jnp.logical_not(first))
            def _(ref=ref, gv=gv):
                ref[...] += gv

    row = pl.BlockSpec((tile, D), lambda i: (i, 0))
    bv = pl.BlockSpec((None, 1, D), lambda i: (i // tpb, 0, 0))
    pv = pl.BlockSpec((1, D), lambda i: (0, 0))
    return pl.pallas_call(
        body, name=name, grid=(T // tile,),
        in_specs=[row, row, row, bv, bv, bv, pv],
        out_specs=[row, row, bv, bv, bv, pv, pl.BlockSpec((1, LANE), lambda i: (0, 0))],
        out_shape=[jax.ShapeDtypeStruct((T, D), F32), jax.ShapeDtypeStruct((T, D), BF16),
                   jax.ShapeDtypeStruct(gate2.shape, F32), jax.ShapeDtypeStruct(gate2.shape, F32),
                   jax.ShapeDtypeStruct(gate2.shape, F32), jax.ShapeDtypeStruct(wf.shape, F32),
                   jax.ShapeDtypeStruct((1, LANE), F32)],
        compiler_params=_cparams("arbitrary"),
    )(x1, m, tgt, gate2, shift, scale, wf)


CONV_ROWS = 128


def _conv_geometry(K):
    pad = -(-(K - 1) // SUBLANE) * SUBLANE
    return pad


def _fill_shifted(dst_ref, src_ref, shifts, length):
    for s in shifts:
        for r0 in range(0, length, CONV_ROWS):
            n = min(CONV_ROWS, length - r0)
            dst_ref[s, pl.ds(r0, n), :] = src_ref[pl.ds(r0 + s, n), :]


def _dwconv_fwd(x, xoff, w, woff, width, seq, name):
    T, K = x.shape[0], w.shape[0]
    CB = LANE
    P = _conv_geometry(K)
    nb, ncb = T // seq, width // CB
    xo, wo = xoff // CB, woff // CB
    offs = [P - (K - 1) + j for j in range(K)]
    shifts = sorted({o % SUBLANE for o in offs})
    RC = min(CONV_ROWS, seq)

    def body(x_ref, w_ref, o_ref, xpad, xs):
        xpad[pl.ds(0, P), :] = jnp.zeros((P, CB), F32)
        xpad[pl.ds(P, seq), :] = x_ref[...]
        xpad[pl.ds(P + seq, SUBLANE), :] = jnp.zeros((SUBLANE, CB), F32)
        _fill_shifted(xs, xpad, shifts, seq + P)
        wv = w_ref[...]

        def chunk(i, carry):
            r0 = pl.multiple_of(i * RC, RC)
            acc = jnp.zeros((RC, CB), F32)
            for j, o in enumerate(offs):
                acc = acc + wv[j:j + 1, :] * xs[o % SUBLANE, pl.ds(r0 + (o // SUBLANE) * SUBLANE, RC), :]
            o_ref[pl.ds(r0, RC), :] = acc
            return carry

        lax.fori_loop(0, seq // RC, chunk, 0)

    return pl.pallas_call(
        body, name=name, grid=(nb, ncb),
        in_specs=[pl.BlockSpec((seq, CB), lambda b, c: (b, xo + c)),
                  pl.BlockSpec((K, CB), lambda b, c: (0, wo + c))],
        out_specs=pl.BlockSpec((seq, CB), lambda b, c: (b, c)),
        out_shape=jax.ShapeDtypeStruct((T, width), F32),
        scratch_shapes=[pltpu.VMEM((seq + P + SUBLANE, CB), F32), pltpu.VMEM((SUBLANE, seq + P, CB), F32)],
        compiler_params=_cparams("parallel", "parallel"),
    )(x, w)


def _dwconv_bwd(x, xoff, w, woff, dy, width, seq, name, dx_dtype):
    T, K = x.shape[0], w.shape[0]
    CB = LANE
    P = _conv_geometry(K)
    nb, ncb = T // seq, width // CB
    xo, wo = xoff // CB, woff // CB
    offs = [P - (K - 1) + j for j in range(K)]
    roffs = [K - 1 - j for j in range(K)]
    xshifts = sorted({o % SUBLANE for o in offs})
    yshifts = sorted({o % SUBLANE for o in roffs})
    RC = min(CONV_ROWS, seq)

    def body(x_ref, w_ref, dy_ref, dx_ref, dw_ref, xpad, xs, ypad, ys, dwacc):
        b = pl.program_id(1)
        xpad[pl.ds(0, P), :] = jnp.zeros((P, CB), F32)
        xpad[pl.ds(P, seq), :] = x_ref[...]
        xpad[pl.ds(P + seq, SUBLANE), :] = jnp.zeros((SUBLANE, CB), F32)
        _fill_shifted(xs, xpad, xshifts, seq + P)
        ypad[pl.ds(0, seq), :] = dy_ref[...].astype(F32)
        ypad[pl.ds(seq, P + SUBLANE), :] = jnp.zeros((P + SUBLANE, CB), F32)
        _fill_shifted(ys, ypad, yshifts, seq + P)
        wv = w_ref[...]

        @pl.when(b == 0)
        def _():
            dwacc[...] = jnp.zeros_like(dwacc)

        def chunk(i, carry):
            r0 = pl.multiple_of(i * RC, RC)
            dyc = ypad[pl.ds(r0, RC), :]
            acc = jnp.zeros((RC, CB), F32)
            for j in range(K):
                o, ro = offs[j], roffs[j]
                acc = acc + wv[j:j + 1, :] * ys[ro % SUBLANE, pl.ds(r0 + (ro // SUBLANE) * SUBLANE, RC), :]
                prod = dyc * xs[o % SUBLANE, pl.ds(r0 + (o // SUBLANE) * SUBLANE, RC), :]
                part = prod[0:SUBLANE, :]
                for q in range(1, RC // SUBLANE):
                    part = part + prod[q * SUBLANE:(q + 1) * SUBLANE, :]
                dwacc[pl.ds(j * SUBLANE, SUBLANE), :] += part
            dx_ref[pl.ds(r0, RC), :] = acc.astype(dx_ref.dtype)
            return carry

        lax.fori_loop(0, seq // RC, chunk, 0)

        @pl.when(b == nb - 1)
        def _():
            for j in range(K):
                dw_ref[j:j + 1, :] = jnp.sum(dwacc[pl.ds(j * SUBLANE, SUBLANE), :], axis=0, keepdims=True)

    return pl.pallas_call(
        body, name=name, grid=(ncb, nb),
        in_specs=[pl.BlockSpec((seq, CB), lambda c, b: (b, xo + c)),
                  pl.BlockSpec((K, CB), lambda c, b: (0, wo + c)),
                  pl.BlockSpec((seq, CB), lambda c, b: (b, c))],
        out_specs=[pl.BlockSpec((seq, CB), lambda c, b: (b, c)),
                   pl.BlockSpec((K, CB), lambda c, b: (0, c))],
        out_shape=[jax.ShapeDtypeStruct((T, width), dx_dtype), jax.ShapeDtypeStruct((K, width), F32)],
        scratch_shapes=[pltpu.VMEM((seq + P + SUBLANE, CB), F32), pltpu.VMEM((SUBLANE, seq + P, CB), F32),
                        pltpu.VMEM((seq + P + SUBLANE, CB), F32), pltpu.VMEM((SUBLANE, seq + P, CB), F32),
                        pltpu.VMEM((K * SUBLANE, CB), F32)],
        compiler_params=_cparams("parallel", "arbitrary"),
    )(x, w, dy)


def _dot_raw(a, b, dims, mode):
    o = a.ndim - 2
    contract = {"nn": ((o + 1,), (o,)), "nt": ((o + 1,), (o + 1,)), "tn": ((o,), (o,))}[dims]
    batch = (tuple(range(o)), tuple(range(o)))

    def d(u, v):
        return lax.dot_general(u, v, (contract, batch), preferred_element_type=F32)

    ah, bh = a.astype(BF16), b.astype(BF16)
    if mode == "bf16":
        return d(ah, bh)
    al, bl = (a - ah.astype(F32)).astype(BF16), (b - bh.astype(F32)).astype(BF16)
    return d(ah, bh) + (d(ah, bl) + d(al, bh))


@functools.partial(jax.custom_vjp, nondiff_argnums=(2, 3))
def _dot_vjp(a, b, dims, mode):
    return _dot_raw(a, b, dims, mode)


def _dot_vjp_fwd(a, b, dims, mode):
    return _dot_raw(a, b, dims, mode), (a, b)


def _dot_vjp_bwd(dims, mode, res, g):
    a, b = res
    if dims == "nn":
        return _dot_raw(g, b, "nt", mode), _dot_raw(a, g, "tn", mode)
    if dims == "nt":
        return _dot_raw(g, b, "nn", mode), _dot_raw(g, a, "tn", mode)
    return _dot_raw(b, g, "nt", mode), _dot_raw(a, g, "nn", mode)


_dot_vjp.defvjp(_dot_vjp_fwd, _dot_vjp_bwd)


def _tri_inv_raw(L):
    C = L.shape[-1]
    shp = (1,) * (L.ndim - 2) + (C, C)
    eye = lax.broadcasted_iota(jnp.int32, shp, L.ndim - 2) == lax.broadcasted_iota(jnp.int32, shp, L.ndim - 1)
    Tm = jnp.where(eye, 1.0, 0.0) - L
    Mp = _dot_raw(L, L, "nn", "x3")
    n_sq = int(np.ceil(np.log2(C))) - 1
    for it in range(n_sq):
        Tm = Tm + _dot_raw(Tm, Mp, "nn", "x3")
        if it + 1 < n_sq:
            Mp = _dot_raw(Mp, Mp, "nn", "x3")
    return Tm


@jax.custom_vjp
def _tri_inv_vjp(L):
    return _tri_inv_raw(L)


def _tri_inv_fwd(L):
    Tm = _tri_inv_raw(L)
    return Tm, Tm


def _tri_inv_bwd(Tm, g):
    return (-_dot_raw(_dot_raw(Tm, g, "tn", "x3"), Tm, "nt", "x3"),)


_tri_inv_vjp.defvjp(_tri_inv_fwd, _tri_inv_bwd)


def _delta_chunk(S, qr, kr, vr, z, lg, a_vec, dt_vec, nw, h0, nheads, diff=False):
    _dot = _dot_vjp if diff else _dot_raw
    _tri_inv = _tri_inv_vjp if diff else _tri_inv_raw
    HB, C, D = qr.shape
    lane = lax.broadcasted_iota(jnp.int32, (HB, 1, LANE), 2)
    head = lax.broadcasted_iota(jnp.int32, (HB, 1, LANE), 0) + h0
    sel_b, sel_a = lane == head, lane == head + nheads

    def pick(sel, mat):
        return jnp.sum(jnp.where(sel, mat[None], 0.0), axis=-1, keepdims=True)

    b_logit, a_logit = pick(sel_b, lg), pick(sel_a, lg)
    a_h, dt_h = pick(sel_b, a_vec), pick(sel_b, dt_vec)

    q, k, v = _silu(qr), _silu(kr), _silu(vr)
    q = q * lax.rsqrt(jnp.sum(q * q, axis=-1, keepdims=True) + EPS) * (D ** -0.5)
    k = k * lax.rsqrt(jnp.sum(k * k, axis=-1, keepdims=True) + EPS)
    beta = _sigmoid(b_logit)
    g = -jnp.exp(a_h) * _softplus(a_logit + dt_h)

    row = lax.broadcasted_iota(jnp.int32, (1, C, C), 1)
    col = lax.broadcasted_iota(jnp.int32, (1, C, C), 2)
    causal, strict, eye = row >= col, row > col, row == col
    tril = jnp.broadcast_to(causal.astype(F32), (HB, C, C))
    gc_d = _dot(tril, jnp.broadcast_to(g, (HB, C, D)), "nn", "x3")
    gi = _dot(tril, jnp.broadcast_to(g, (HB, C, C)), "nn", "x3")
    gj = _dot(jnp.ones((HB, C, C), F32), jnp.where(eye, gi, 0.0), "nn", "x3")
    gtot = jnp.sum(g, axis=1, keepdims=True)
    decay = jnp.exp(jnp.where(causal, gi - gj, -1e30))
    e_gc = jnp.exp(gc_d)

    kb, vb = k * beta, v * beta
    L = jnp.where(strict, _dot(kb, k, "nt", "bf16") * decay, 0.0)
    Tm = _tri_inv(L)
    u = _dot(Tm, vb, "nn", "x3")
    w = _dot(Tm, kb * e_gc, "nn", "x3")
    attn = jnp.where(causal, _dot(q, k, "nt", "bf16") * decay, 0.0)
    v_new = u - _dot(w, S, "nn", "bf16")
    o = _dot(q * e_gc, S, "nn", "bf16") + _dot(attn, v_new, "nn", "bf16")
    S_new = S * jnp.exp(gtot) + _dot(k * jnp.exp(gtot - gc_d), v_new, "tn", "bf16")
    y = _rms(o, nw) * _silu(z)
    return y, S_new


def _delta_specs(DN, CC, HP, NC, rev):
    W = HP * HEAD_DIM
    nb = (lambda n: NC - 1 - n) if rev else (lambda n: n)
    qkv = [pl.BlockSpec((CHUNK, W), lambda b, g, n, p=p: (b * NC + nb(n), p * (DN // W) + g)) for p in range(3)]
    z = pl.BlockSpec((CHUNK, W), lambda b, g, n: (b * NC + nb(n), (2 * CC + 3 * DN) // W + g))
    lg = pl.BlockSpec((CHUNK, LANE), lambda b, g, n: (b * NC + nb(n), (2 * CC + 4 * DN) // LANE))
    pv = pl.BlockSpec((1, LANE), lambda b, g, n: (0, 0))
    return qkv, z, lg, pv, nb


def _delta_fwd(qkv_c, proj, a_vec, dt_vec, nw, DN, CC, seq, name, HP):
    T = qkv_c.shape[0]
    H, NC, Bl = DN // HEAD_DIM, seq // CHUNK, T // seq
    W = HP * HEAD_DIM
    qkv, zs, lgs, pv, nb = _delta_specs(DN, CC, HP, NC, False)

    def body(q_ref, k_ref, v_ref, z_ref, lg_ref, a_ref, dt_ref, nw_ref, y_ref, ssave_ref, S_ref):
        n, grp = pl.program_id(2), pl.program_id(1)

        @pl.when(n == 0)
        def _():
            S_ref[...] = jnp.zeros_like(S_ref)

        lg, av, dv, nv = lg_ref[...], a_ref[...], dt_ref[...], nw_ref[...]
        sls = [slice(hh * HEAD_DIM, (hh + 1) * HEAD_DIM) for hh in range(HP)]
        heads = lambda ref: jnp.stack([ref[:, sl].astype(F32) for sl in sls])
        S = S_ref[...]
        ssave_ref[...] = S
        y, S_new = _delta_chunk(S, heads(q_ref), heads(k_ref), heads(v_ref), heads(z_ref),
                                lg, av, dv, nv, grp * HP, H)
        for hh, sl in enumerate(sls):
            y_ref[:, sl] = y[hh].astype(y_ref.dtype)
        S_ref[...] = S_new

    return pl.pallas_call(
        body, name=name, grid=(Bl, H // HP, NC),
        in_specs=[*qkv, zs, lgs, pv, pv, pv],
        out_specs=[pl.BlockSpec((CHUNK, W), lambda b, g, n: (b * NC + n, g)),
                   pl.BlockSpec((None, HP, None, HEAD_DIM, HEAD_DIM), lambda b, g, n: (b, g, n, 0, 0))],
        out_shape=[jax.ShapeDtypeStruct((T, DN), BF16),
                   jax.ShapeDtypeStruct((Bl, H, NC, HEAD_DIM, HEAD_DIM), F32)],
        scratch_shapes=[pltpu.VMEM((HP, HEAD_DIM, HEAD_DIM), F32)],
        compiler_params=_cparams("parallel", "parallel", "arbitrary"),
    )(qkv_c, qkv_c, qkv_c, proj, proj, a_vec, dt_vec, nw)


def _delta_bwd(qkv_c, proj, a_vec, dt_vec, nw, ssave, dy, dyoff, DN, CC, seq, name, HP, comm=None):
    T = qkv_c.shape[0]
    H, NC, Bl = DN // HEAD_DIM, seq // CHUNK, T // seq
    W, G = HP * HEAD_DIM, DN // (HP * HEAD_DIM)
    qkv, zs, lgs, pv, nb = _delta_specs(DN, CC, HP, NC, True)
    n_ci, n_co = (len(comm.ins), len(comm.out_shapes)) if comm is not None else (0, 0)
    n_in, n_out = 10, 8

    def body(*refs):
        q_ref, k_ref, v_ref, z_ref, lg_ref, a_ref, dt_ref, nw_ref, ss_ref, dy_ref = refs[:n_in]
        comm_in = refs[n_in:n_in + n_ci]
        pos = n_in + n_ci
        dq_ref, dk_ref, dv_ref, dz_ref, dlg_ref, da_ref, ddt_ref, dnw_ref = refs[pos:pos + n_out]
        comm_out = refs[pos + n_out:pos + n_out + n_co]
        dS_ref, comm_sems = refs[pos + n_out + n_co], refs[pos + n_out + n_co + 1:]
        b, grp, n = pl.program_id(0), pl.program_id(1), pl.program_id(2)

        if comm is not None:
            @pl.when(jnp.logical_and(jnp.logical_and(b == 0, grp == 0), n == 0))
            def _():
                comm.start(comm_in, comm_out, comm_sems)

        @pl.when(n == 0)
        def _():
            dS_ref[...] = jnp.zeros_like(dS_ref)

        lg, av, dv, nv = lg_ref[...], a_ref[...], dt_ref[...], nw_ref[...]
        sls = [slice(hh * HEAD_DIM, (hh + 1) * HEAD_DIM) for hh in range(HP)]
        heads = lambda ref: jnp.stack([ref[:, sl].astype(F32) for sl in sls])
        fn = functools.partial(_delta_chunk, h0=grp * HP, nheads=H, diff=True)
        _, vjp = jax.vjp(fn, ss_ref[...], heads(q_ref), heads(k_ref), heads(v_ref), heads(z_ref), lg, av, dv, nv)
        gS, gq, gk, gv, gz, glg, ga, gdt, gnw = vjp((heads(dy_ref), dS_ref[...]))
        dS_ref[...] = gS
        for hh, sl in enumerate(sls):
            dq_ref[:, sl] = gq[hh].astype(dq_ref.dtype)
            dk_ref[:, sl] = gk[hh].astype(dk_ref.dtype)
            dv_ref[:, sl] = gv[hh].astype(dv_ref.dtype)
            dz_ref[:, sl] = gz[hh].astype(dz_ref.dtype)
        dlg_ref[...] = glg
        dpar = [ga, gdt, gnw]
        first = jnp.logical_and(jnp.logical_and(b == 0, grp == 0), n == 0)
        for ref, gv_ in zip((da_ref, ddt_ref, dnw_ref), dpar):
            @pl.when(first)
            def _(ref=ref, gv_=gv_):
                ref[...] = gv_

            @pl.when(jnp.logical_not(first))
            def _(ref=ref, gv_=gv_):
                ref[...] += gv_

        if comm is not None:
            @pl.when(jnp.logical_and(jnp.logical_and(b == Bl - 1, grp == G - 1), n == NC - 1))
            def _():
                comm.finish(comm_in, comm_out, comm_sems)

    rowb = pl.BlockSpec((CHUNK, W), lambda b, g, n: (b * NC + nb(n), g))
    res = pl.pallas_call(
        body, name=name, grid=(Bl, G, NC),
        in_specs=[*qkv, zs, lgs, pv, pv, pv,
                  pl.BlockSpec((None, HP, None, HEAD_DIM, HEAD_DIM), lambda b, g, n: (b, g, nb(n), 0, 0)),
                  pl.BlockSpec((CHUNK, W), lambda b, g, n: (b * NC + nb(n), dyoff // W + g))] + [_any_spec()] * n_ci,
        out_specs=[rowb, rowb, rowb, rowb,
                   pl.BlockSpec((None, CHUNK, LANE), lambda b, g, n: (g, b * NC + nb(n), 0)),
                   pv, pv, pv] + [_any_spec()] * n_co,
        out_shape=[jax.ShapeDtypeStruct((T, DN), F32)] * 3 + [jax.ShapeDtypeStruct((T, DN), BF16),
                   jax.ShapeDtypeStruct((G, T, LANE), F32)] + [jax.ShapeDtypeStruct((1, LANE), F32)] * 3
                  + (comm.out_shapes if comm is not None else []),
        input_output_aliases={n_in + ci: n_out + co for ci, co in comm.aliases.items()} if comm is not None else {},
        scratch_shapes=[pltpu.VMEM((HP, HEAD_DIM, HEAD_DIM), F32)] + (comm.sems if comm is not None else []),
        compiler_params=_cparams("arbitrary", "arbitrary", "arbitrary"),
    )(qkv_c, qkv_c, qkv_c, proj, proj, a_vec, dt_vec, nw, ssave, dy, *(comm.ins if comm is not None else []))
    return list(res[:n_out]), list(res[n_out:])


def _place():
    x, y, c = lax.axis_index("x"), lax.axis_index("y"), lax.axis_index("c")
    chips = [(1 - x, y), (x, 1 - y), (1 - x, 1 - y)]
    return x, y, c, chips


class _Comm:
    def __init__(self, ins, out_shapes, aliases, sems, start, finish):
        self.ins, self.out_shapes, self.aliases, self.sems = list(ins), list(out_shapes), dict(aliases), list(sems)
        self.start, self.finish = start, finish


def _run_comm(comm, name):
    ni, no = len(comm.ins), len(comm.out_shapes)

    def body(*refs):
        ins, outs, sems = refs[:ni], refs[ni:ni + no], refs[ni + no:]
        comm.start(ins, outs, sems)
        comm.finish(ins, outs, sems)

    return pl.pallas_call(
        body, name=name,
        in_specs=[_any_spec()] * ni, out_specs=[_any_spec()] * no, out_shape=comm.out_shapes,
        input_output_aliases=comm.aliases, scratch_shapes=comm.sems,
    )(*comm.ins)


def _ag_comm(shards):
    n = len(shards)

    def parts(ins, outs, sems):
        send_sems, recv_sems, local_sems = sems
        x, y, c, chips = _place()
        me, sibling = (x, y, c), (x, y, 1 - c)

        def copy(a, k, block, to, src=None):
            dst = outs[a].at[4 * block[0] + 2 * block[1] + block[2]]
            return pltpu.make_async_remote_copy(
                src_ref=dst if src is None else src, dst_ref=dst,
                send_sem=send_sems.at[a * 7 + k], recv_sem=recv_sems.at[a * 7 + k],
                device_id=to, device_id_type=MESH)

        mine = [pltpu.make_async_copy(ins[a], outs[a].at[4 * x + 2 * y + c], local_sems.at[a]) for a in range(n)]
        first = []
        for a in range(n):
            first.append(copy(a, 0, me, sibling, src=ins[a]))
            first += [copy(a, 1 + j, me, (*chip, c), src=ins[a]) for j, chip in enumerate(chips)]
        return c, chips, me, sibling, copy, mine, first

    def start(ins, outs, sems):
        _, _, _, _, _, mine, first = parts(ins, outs, sems)
        for cp in mine + first:
            cp.start()

    def finish(ins, outs, sems):
        c, chips, me, sibling, copy, mine, first = parts(ins, outs, sems)
        passed = []
        for j, chip in enumerate(chips):
            for a in range(n):
                copy(a, 1 + j, (*chip, c), me).wait_recv()
                fwd = copy(a, 4 + j, (*chip, c), sibling)
                fwd.start()
                passed.append(fwd)
        for a in range(n):
            copy(a, 0, sibling, me).wait_recv()
            for j, chip in enumerate(chips):
                copy(a, 4 + j, (*chip, 1 - c), me).wait_recv()
        for cp in first + passed:
            cp.wait_send()
        for cp in mine:
            cp.wait()

    return _Comm(shards, [jax.ShapeDtypeStruct((N_DEV,) + s.shape, s.dtype) for s in shards], {},
                 [pltpu.SemaphoreType.DMA((7 * n,)), pltpu.SemaphoreType.DMA((7 * n,)), pltpu.SemaphoreType.DMA((n,))],
                 start, finish)


def _all_gather(shards, name):
    return _run_comm(_ag_comm(shards), name)


def _sib_comm(grads):
    n = len(grads)

    def copies(ins, outs, sems):
        send_sems, recv_sems = sems
        x, y, c, _ = _place()
        return [pltpu.make_async_remote_copy(
            src_ref=ins[a].at[:, 1 - c], dst_ref=outs[a],
            send_sem=send_sems.at[a], recv_sem=recv_sems.at[a],
            device_id=(x, y, 1 - c), device_id_type=MESH) for a in range(n)]

    def start(ins, outs, sems):
        for cp in copies(ins, outs, sems):
            cp.start()

    def finish(ins, outs, sems):
        for cp in copies(ins, outs, sems):
            cp.wait()

    return _Comm(grads, [jax.ShapeDtypeStruct((N_CHIP,) + g.shape[2:], g.dtype) for g in grads], {},
                 [pltpu.SemaphoreType.DMA((n,)), pltpu.SemaphoreType.DMA((n,))], start, finish)


def _chip_comm(parts, lands, layer):
    n = len(parts)

    def copies(refs_in, outs, sems):
        ins = refs_in[:n]
        send_sems, recv_sems, local_sems = sems
        x, y, c, chips = _place()
        myq = 2 * x + y
        mine = [pltpu.make_async_copy(ins[a].at[myq], outs[a].at[layer, myq], local_sems.at[a]) for a in range(n)]
        sends, recvs = [], []
        for a in range(n):
            for j, (cx, cy) in enumerate(chips):
                kw = dict(send_sem=send_sems.at[a * 3 + j], recv_sem=recv_sems.at[a * 3 + j],
                          device_id=(cx, cy, c), device_id_type=MESH)
                sends.append(pltpu.make_async_remote_copy(
                    src_ref=ins[a].at[2 * cx + cy], dst_ref=outs[a].at[layer, myq], **kw))
                recvs.append(pltpu.make_async_remote_copy(
                    src_ref=ins[a].at[2 * cx + cy], dst_ref=outs[a].at[layer, 2 * cx + cy], **kw))
        return mine, sends, recvs

    def start(ins, outs, sems):
        mine, sends, _ = copies(ins, outs, sems)
        for cp in mine + sends:
            cp.start()

    def finish(ins, outs, sems):
        mine, sends, recvs = copies(ins, outs, sems)
        for cp in recvs:
            cp.wait_recv()
        for cp in sends:
            cp.wait_send()
        for cp in mine:
            cp.wait()

    return _Comm(list(parts) + list(lands), [jax.ShapeDtypeStruct(l.shape, l.dtype) for l in lands],
                 {n + a: a for a in range(n)},
                 [pltpu.SemaphoreType.DMA((3 * n,)), pltpu.SemaphoreType.DMA((3 * n,)), pltpu.SemaphoreType.DMA((n,))],
                 start, finish)


def _pair_sum(g, land, name, tr=256):
    _, _, R, C = g.shape
    tr = _tile(R, tr, SUBLANE)
    core = lax.axis_index("c").astype(jnp.int32).reshape(1)

    def body(c_ref, g_ref, l_ref, o_ref):
        o_ref[...] = (g_ref[...] + l_ref[...]).astype(o_ref.dtype)

    return pl.pallas_call(
        body, name=name,
        grid_spec=pltpu.PrefetchScalarGridSpec(
            num_scalar_prefetch=1, grid=(N_CHIP, R // tr),
            in_specs=[pl.BlockSpec((None, None, tr, C), lambda q, i, cr: (q, cr[0], i, 0)),
                      pl.BlockSpec((None, tr, C), lambda q, i, cr: (q, i, 0))],
            out_specs=pl.BlockSpec((None, tr, C), lambda q, i, cr: (q, i, 0))),
        out_shape=jax.ShapeDtypeStruct((N_CHIP, R, C), BF16),
        compiler_params=_cparams("parallel", "parallel"),
    )(core, g, land)


def _adam_math(w, g, m, v):
    m = ADAM_B1 * m + (1.0 - ADAM_B1) * g
    v = ADAM_B2 * v + (1.0 - ADAM_B2) * jnp.square(g)
    m_hat = m / (1.0 - ADAM_B1 ** ADAM_STEP)
    v_hat = v / (1.0 - ADAM_B2 ** ADAM_STEP)
    delta = -ADAM_LR * (m_hat / (jnp.sqrt(v_hat) + ADAM_EPS) + ADAM_WD * w)
    return delta, m, v


def _adam_landed(land, w, m, v, name, tr=256):
    L, _, R, C = land.shape
    tr = _tile(R, tr, SUBLANE)

    def body(l_ref, w_ref, m_ref, v_ref, g_ref, d_ref, nm_ref, nv_ref):
        g = l_ref[0].astype(F32)
        for q in range(1, N_CHIP):
            g = g + l_ref[q].astype(F32)
        d, nm, nv = _adam_math(w_ref[...], g, m_ref[...], v_ref[...])
        g_ref[...], d_ref[...], nm_ref[...], nv_ref[...] = g, d, nm, nv

    blk = pl.BlockSpec((None, tr, C), lambda l, i: (l, i, 0))
    return pl.pallas_call(
        body, name=name, grid=(L, R // tr),
        in_specs=[pl.BlockSpec((None, N_CHIP, tr, C), lambda l, i: (l, 0, i, 0)), blk, blk, blk],
        out_specs=[blk] * 4, out_shape=[jax.ShapeDtypeStruct((L, R, C), F32)] * 4,
        compiler_params=_cparams("parallel", "parallel"),
    )(land, w, m, v)


def _adam(g, w, m, v, name, tr=256):
    R, C = w.shape
    tr = _tile(R, tr, SUBLANE)

    def body(g_ref, w_ref, m_ref, v_ref, d_ref, nm_ref, nv_ref):
        d_ref[...], nm_ref[...], nv_ref[...] = _adam_math(w_ref[...], g_ref[...], m_ref[...], v_ref[...])

    blk = pl.BlockSpec((tr, C), lambda i: (i, 0))
    return pl.pallas_call(
        body, name=name, grid=(R // tr,),
        in_specs=[blk] * 4, out_specs=[blk] * 3, out_shape=[jax.ShapeDtypeStruct((R, C), F32)] * 3,
        compiler_params=_cparams("parallel"),
    )(g, w, m, v)


def _bias_sum(a, name, tn=2048):
    B, N = a.shape
    tn = _tile(N, tn)

    def body(a_ref, o_ref):
        o_ref[...] = jnp.sum(a_ref[...], axis=0, keepdims=True)

    return pl.pallas_call(
        body, name=name, grid=(N // tn,),
        in_specs=[pl.BlockSpec((B, tn), lambda j: (0, j))],
        out_specs=pl.BlockSpec((1, tn), lambda j: (0, j)),
        out_shape=jax.ShapeDtypeStruct((1, N), F32),
        compiler_params=_cparams("parallel"),
    )(a)


def _sum_devices(stack, name):
    _, R, C = stack.shape

    def body(s_ref, o_ref):
        acc = s_ref[0]
        for d in range(1, N_DEV):
            acc = acc + s_ref[d]
        o_ref[...] = acc

    return pl.pallas_call(
        body, name=name, grid=(1,),
        in_specs=[pl.BlockSpec((N_DEV, R, C), lambda i: (0, 0, 0))],
        out_specs=pl.BlockSpec((R, C), lambda i: (0, 0)),
        out_shape=jax.ShapeDtypeStruct((R, C), F32),
    )(stack)


def _pack(arrs):
    flat = jnp.concatenate([a.reshape(-1) for a in arrs])
    n = flat.shape[0]
    rows = -(-n // (LANE * SUBLANE)) * SUBLANE
    flat = jnp.pad(flat, (0, rows * LANE - n))
    return flat.reshape(rows, LANE)


def _unpack(packed, shapes):
    flat = packed.reshape(-1)
    out, pos = [], 0
    for s in shapes:
        n = int(np.prod(s))
        out.append(flat[pos:pos + n].reshape(s))
        pos += n
    return out


def _cols_to_shards(g, n_cols):
    R = g.shape[0]
    return g[:, :n_cols].reshape(R, N_DEV, n_cols // N_DEV).transpose(1, 0, 2).reshape(N_CHIP, 2, R, n_cols // N_DEV)


def _rows_to_shards(g):
    R, C = g.shape
    return g.reshape(N_CHIP, 2, R // N_DEV, C)


def _shards_to_cols(w8, pad_to=None):
    _, R, Cs = w8.shape
    w = w8.transpose(1, 0, 2).reshape(R, N_DEV * Cs)
    if pad_to is not None and pad_to > N_DEV * Cs:
        w = jnp.pad(w, ((0, 0), (0, pad_to - N_DEV * Cs)))
    return w


def kernel(x, c, w_ada, b_ada, norm1_w, w_in, conv_dw_w, conv_dw_b, conv_ln_w, conv_ln_b, w_pw2, conv_out_norm_w, qkv_conv_w, a_log, dt_bias, dn_norm_w, w_out, norm2_w, w_up, w_down, final_ada_w, final_ada_b, final_norm_w, loss_target, m_w_ada, m_b_ada, m_norm1_w, m_w_in, m_conv_dw_w, m_conv_dw_b, m_conv_ln_w, m_conv_ln_b, m_w_pw2, m_conv_out_norm_w, m_qkv_conv_w, m_a_log, m_dt_bias, m_dn_norm_w, m_w_out, m_norm2_w, m_w_up, m_w_down, m_final_ada_w, m_final_ada_b, m_final_norm_w, v_w_ada, v_b_ada, v_norm1_w, v_w_in, v_conv_dw_w, v_conv_dw_b, v_conv_ln_w, v_conv_ln_b, v_w_pw2, v_conv_out_norm_w, v_qkv_conv_w, v_a_log, v_dt_bias, v_dn_norm_w, v_w_out, v_norm2_w, v_w_up, v_w_down, v_final_ada_w, v_final_ada_b, v_final_norm_w):
    Bl, S, D = x.shape
    T = Bl * S
    L = w_ada.shape[0]
    CC = conv_ln_w.shape[1]
    DN = qkv_conv_w.shape[2] * N_DEV // 3
    H = a_log.shape[1]
    IN_COLS = w_in.shape[2] * N_DEV
    IN_PAD = -(-IN_COLS // LANE) * LANE
    DFF = w_up.shape[2] * N_DEV
    NMOD = w_ada.shape[2] * N_DEV // D
    ada_cols = w_ada.shape[2]
    fada_cols = final_ada_w.shape[1]
    HP = min(8, H)
    assert DN == H * HEAD_DIM and NMOD == 6 and S % CHUNK == 0

    dev = 4 * lax.axis_index("x") + 2 * lax.axis_index("y") + lax.axis_index("c")
    xf = x.reshape(T, D)
    tgt = loss_target.reshape(T, D)

    c8, cdw8, qcw8 = _all_gather([c, conv_dw_w, qkv_conv_w], "ag_small")
    c_all = c8.reshape(N_DEV * Bl, D)
    conv_w_full = cdw8.transpose(1, 2, 0, 3).reshape(L, conv_dw_w.shape[1], CC)
    qkv_w_full = qcw8.transpose(1, 2, 0, 3).reshape(L, qkv_conv_w.shape[1], 3 * DN)
    (c_act,) = _rowwise(lambda cv: (_silu(cv),), [(c_all, 0, D)], [], [], [(D, F32)], "c_act", tile=N_DEV * Bl)
    mod_cols = []
    for l in range(L):
        bias = lax.dynamic_slice_in_dim(b_ada[l], dev * ada_cols, ada_cols).reshape(1, ada_cols)
        mod_cols.append(_matmul(c_act, w_ada, "nn", f"mod{l}", bias=bias, b_layer=l, tk=2048))
    bias_f = lax.dynamic_slice_in_dim(final_ada_b, dev * fada_cols, fada_cols).reshape(1, fada_cols)
    mod_cols.append(_matmul(c_act, final_ada_w, "nn", "modf", bias=bias_f, tk=2048))
    (mod8,) = _all_gather([jnp.concatenate(mod_cols, axis=1)], "ag_mod")
    mod8 = lax.dynamic_slice_in_dim(mod8, dev * Bl, Bl, axis=1)
    mod_l = mod8[:, :, :L * ada_cols].reshape(N_DEV, Bl, L, ada_cols).transpose(1, 2, 0, 3).reshape(Bl, L, NMOD, 1, D)
    mod_f = mod8[:, :, L * ada_cols:].transpose(1, 0, 2).reshape(Bl, 2, 1, D)
    shift_f, scale_f = mod_f[:, 0], mod_f[:, 1]

    def modv(l, k):
        return mod_l[:, l, k]

    pv = lambda a: a.reshape(1, -1)
    padl = lambda a: jnp.pad(a, (0, LANE - a.shape[0])).reshape(1, LANE)

    saved = []
    wts = []
    x1p = mp = None
    big = ("w_in", "w_pw2", "w_out", "w_up", "w_down")
    wsrc = dict(w_in=w_in, w_pw2=w_pw2, w_out=w_out, w_up=w_up, w_down=w_down)
    shard16 = lambda k, l: wsrc[k][l].astype(BF16)
    g8 = dict(zip(big, _all_gather([shard16(k, 0) for k in big], "ag_w0")))
    for l in range(L):
        W = dict(w_in=_shards_to_cols(g8["w_in"], IN_PAD), w_pw2=g8["w_pw2"].reshape(CC, CC),
                 w_out=g8["w_out"].reshape(D, D), w_up=_shards_to_cols(g8["w_up"]), w_down=g8["w_down"].reshape(DFF, D))
        wts.append(W)
        sv = {}
        g8 = {}

        def _matmul_ag(*args, ag, l=l, g8=g8, **kw):
            if l + 1 == L:
                return _matmul(*args, **kw)
            main, outs = _matmul(*args, comm=_ag_comm([shard16(k, l + 1) for k in ag]), **kw)
            g8.update(zip(ag, outs))
            return main

        if l == 0:
            (h,) = _rowwise(lambda *v: (_f_first(*v)[1],), [(xf, 0, D)], [modv(l, 0), modv(l, 1)], [pv(norm1_w[l])],
                            [(D, BF16)], f"pre{l}")
            xl = xf
        else:
            xl, h = _rowwise(_f_join, [(x1p, 0, D), (mp, 0, D)], [modv(l - 1, 5), modv(l, 0), modv(l, 1)],
                             [pv(norm1_w[l])], [(D, F32), (D, BF16)], f"pre{l}")
        proj = _matmul_ag(h, W["w_in"], "nn", f"proj{l}", ag=("w_down",))
        (u0,) = _rowwise(_f_glu, [(proj, 0, CC), (proj, 1, CC)], [], [], [(CC, F32)], f"glu{l}")
        u1 = _dwconv_fwd(u0, 0, conv_w_full[l], 0, CC, S, f"conv{l}")
        cpv = [pv(conv_dw_b[l]), pv(conv_ln_w[l]), pv(conv_ln_b[l])]
        (u2,) = _rowwise(_f_convmid, [(u1, 0, CC)], [], cpv, [(CC, BF16)], f"convmid{l}")
        u3 = _matmul(u2, W["w_pw2"], "nn", f"pw2{l}")
        (y_conv,) = _rowwise(_f_rms, [(u3, 0, CC)], [], [pv(conv_out_norm_w[l])], [(CC, BF16)], f"convout{l}")
        qkv_c = _dwconv_fwd(proj, 2 * CC, qkv_w_full[l], 0, 3 * DN, S, f"qkvconv{l}")
        dvec = [padl(a_log[l]), padl(dt_bias[l]), pv(dn_norm_w[l])]
        y_dn, ssave = _delta_fwd(qkv_c, proj, *dvec, DN, CC, S, f"delta{l}", HP)
        ycat = jnp.concatenate([y_conv, y_dn], axis=1)
        y = _matmul_ag(ycat, W["w_out"], "nn", f"out{l}", ag=("w_pw2", "w_out"))
        x1, h2 = _rowwise(_f_mid, [(xl, 0, D), (y, 0, D)], [modv(l, 2), modv(l, 3), modv(l, 4)], [pv(norm2_w[l])],
                          [(D, F32), (D, BF16)], f"mid{l}")
        a_act, r_act = _matmul_ag(h2, W["w_up"], "nn", f"up{l}", out_dtypes=(BF16, BF16), ag=("w_up",),
                                  epi=lambda acc: (acc, jnp.square(jnp.maximum(acc, 0.0))))
        m = _matmul_ag(r_act, W["w_down"], "nn", f"down{l}", ag=("w_in",))
        sv.update(xl=xl, h=h, proj=proj, u0=u0, u1=u1, u2=u2, u3=u3, qkv_c=qkv_c, ssave=ssave, ycat=ycat, y=y,
                  x1=x1, h2=h2, a=a_act, r=r_act, m=m, x1p=x1p, mp=mp, cpv=cpv, dvec=dvec)
        saved.append(sv)
        x1p, mp = x1, m

    dx1, dm, dgate2, dshift_f, dscale_f, dwf, loss_part = _final_loss(
        x1p, mp, tgt, modv(L - 1, 5), shift_f, scale_f, pv(final_norm_w), "final")

    wstack = dict(w_in=(w_in, m_w_in, v_w_in), w_pw2=(w_pw2, m_w_pw2, v_w_pw2), w_out=(w_out, m_w_out, v_w_out),
                  w_up=(w_up, m_w_up, v_w_up), w_down=(w_down, m_w_down, v_w_down))
    lands = [jnp.zeros((L, N_CHIP) + wstack[k][0].shape[1:], BF16) for k in big]
    dmod = [[None] * NMOD for _ in range(L)]
    small = {k: [None] * L for k in ("norm1_w", "conv_dw_w", "conv_dw_b", "conv_ln_w", "conv_ln_b", "conv_out_norm_w",
                                     "qkv_conv_w", "a_log", "dt_bias", "dn_norm_w", "norm2_w")}
    N_EARLY = 3
    pending = None
    for l in reversed(range(L)):
        sv, W = saved[l], wts[l]
        dmod[l][5] = dgate2
        da_kw = dict(out_dtypes=(BF16,), sides=(sv["a"],),
                     epi=lambda acc, a: (acc * (2.0 * jnp.maximum(a.astype(F32), 0.0)),))
        if pending is None:
            da = _matmul(dm, W["w_down"], "nt", f"d_r{l}", **da_kw)
            g_down = _matmul(sv["r"], dm, "tn", f"g_down{l}")
            late_comm = None
        else:
            da, sib = _matmul(dm, W["w_down"], "nt", f"d_r{l}", comm=_sib_comm(pending), **da_kw)
            parts = [_pair_sum(g, s, f"rs_sum{l + 1}_{k}") for k, (g, s) in enumerate(zip(pending, sib))]
            g_down, lands[:N_EARLY] = _matmul(sv["r"], dm, "tn", f"g_down{l}",
                                              comm=_chip_comm(parts[:N_EARLY], lands[:N_EARLY], l + 1))
            late_comm = _chip_comm(parts[N_EARLY:], lands[N_EARLY:], l + 1)
        dh2 = _matmul(da, W["w_up"], "nt", f"d_h2{l}")
        g_up = _matmul(sv["h2"], da, "tn", f"g_up{l}")
        (dxl, dy), (dmod[l][2], dmod[l][3], dmod[l][4]), (small["norm2_w"][l],) = _rowwise_bwd(
            _f_mid, [(sv["xl"], 0, D), (sv["y"], 0, D)], [modv(l, 2), modv(l, 3), modv(l, 4)], [pv(norm2_w[l])],
            [(dx1, 0, D), (dh2, 0, D)], [F32, BF16], f"mid_b{l}")
        dycat = _matmul(dy, W["w_out"], "nt", f"d_ycat{l}")
        g_out = _matmul(sv["ycat"], dy, "tn", f"g_out{l}")
        (du3,), _, (small["conv_out_norm_w"][l],) = _rowwise_bwd(
            _f_rms, [(sv["u3"], 0, CC)], [], [pv(conv_out_norm_w[l])], [(dycat, 0, CC)], [BF16], f"convout_b{l}", tile=256)
        du2 = _matmul(du3, W["w_pw2"], "nt", f"d_u2{l}")
        g_pw2 = _matmul(sv["u2"], du3, "tn", f"g_pw2{l}")
        (du1,), _, (small["conv_dw_b"][l], small["conv_ln_w"][l], small["conv_ln_b"][l]) = _rowwise_bwd(
            _f_convmid, [(sv["u1"], 0, CC)], [], sv["cpv"], [(du2, 0, CC)], [F32], f"convmid_b{l}", tile=256)
        du0, small["conv_dw_w"][l] = _dwconv_bwd(sv["u0"], 0, conv_w_full[l], 0, du1, CC, S, f"conv_b{l}", F32)
        (dval, dgate), _, _ = _rowwise_bwd(_f_glu, [(sv["proj"], 0, CC), (sv["proj"], 1, CC)], [], [],
                                           [(du0, 0, CC)], [BF16, BF16], f"glu_b{l}", tile=256)
        (dq_c, dk_c, dv_c, dz, dlg, g_alog, g_dt, small["dn_norm_w"][l]), late = _delta_bwd(
            sv["qkv_c"], sv["proj"], *sv["dvec"], sv["ssave"], dycat, CC, DN, CC, S, f"delta_b{l}", HP, comm=late_comm)
        if late_comm is not None:
            lands[N_EARLY:] = late
        small["a_log"][l], small["dt_bias"][l] = g_alog[0, :H], g_dt[0, :H]
        dqkv, gqw = [], []
        for p, dpc in enumerate((dq_c, dk_c, dv_c)):
            dpart, gw = _dwconv_bwd(sv["proj"], 2 * CC + p * DN, qkv_w_full[l], p * DN, dpc, DN, S,
                                    f"qkvconv_b{l}_{p}", BF16)
            dqkv.append(dpart)
            gqw.append(gw)
        small["qkv_conv_w"][l] = jnp.concatenate(gqw, axis=1)
        (dlog,) = _rowwise(lambda *gs: (sum(gs[1:], gs[0]),), [(dlg[g], 0, LANE) for g in range(dlg.shape[0])], [], [],
                           [(LANE, BF16)], f"dlog{l}")
        pieces = [dval, dgate, *dqkv, dz, dlog]
        dproj = jnp.concatenate(pieces, axis=1)
        dh = _matmul(dproj, W["w_in"], "nt", f"d_h{l}")
        g_in = _matmul(sv["h"], dproj, "tn", f"g_in{l}")
        if l == 0:
            (grad_x,), (dmod[l][0], dmod[l][1]), (small["norm1_w"][l],) = _rowwise_bwd(
                _f_first, [(xf, 0, D)], [modv(l, 0), modv(l, 1)], [pv(norm1_w[l])],
                [(dxl, 0, D), (dh, 0, D)], [F32], f"pre_b{l}")
        else:
            (dx1, dm), (dgate2, dmod[l][0], dmod[l][1]), (small["norm1_w"][l],) = _rowwise_bwd(
                _f_join, [(sv["x1p"], 0, D), (sv["mp"], 0, D)], [modv(l - 1, 5), modv(l, 0), modv(l, 1)],
                [pv(norm1_w[l])], [(dxl, 0, D), (dh, 0, D)], [F32, BF16], f"pre_b{l}")
        pending = [_cols_to_shards(g_in, IN_COLS), _rows_to_shards(g_pw2), _rows_to_shards(g_out),
                   _cols_to_shards(g_up, DFF), _rows_to_shards(g_down)]
    sib = _run_comm(_sib_comm(pending), "rs_sib0")
    parts = [_pair_sum(g, s, f"rs_sum0_{k}") for k, (g, s) in enumerate(zip(pending, sib))]
    lands = _run_comm(_chip_comm(parts, lands, 0), "rs_chip0")

    small_list = [jnp.stack(small[k]) for k in ("norm1_w", "conv_dw_w", "conv_dw_b", "conv_ln_w", "conv_ln_b",
                                                "conv_out_norm_w", "qkv_conv_w", "a_log", "dt_bias", "dn_norm_w",
                                                "norm2_w")]
    small_list += [dwf.reshape(-1), loss_part[0, :1]]
    small_shapes = [a.shape for a in small_list]
    dmod_local = jnp.concatenate(
        [jnp.concatenate([dmod[l][k].reshape(Bl, D) for k in range(NMOD)], axis=1) for l in range(L)]
        + [dshift_f.reshape(Bl, D), dscale_f.reshape(Bl, D)], axis=1)
    packed8, dmod8 = _all_gather([_pack(small_list), dmod_local], "ag_grads")
    summed = _unpack(_sum_devices(packed8, "sum_small"), small_shapes)
    (g_norm1, g_cdw_full, g_cdb, g_clw, g_clb, g_con, g_qcw_full, g_alog, g_dtb, g_dnw, g_norm2, g_fnw, loss) = summed
    loss = loss.reshape(())
    g_cdw = lax.dynamic_slice_in_dim(g_cdw_full, dev * conv_dw_w.shape[2], conv_dw_w.shape[2], axis=2)
    g_qcw = lax.dynamic_slice_in_dim(g_qcw_full, dev * qkv_conv_w.shape[2], qkv_conv_w.shape[2], axis=2)

    dmod_all = dmod8.reshape(N_DEV * Bl, L * NMOD * D + 2 * D)
    g_w_ada, g_b_ada = [], []
    for l in range(L):
        dm_l = dmod_all[:, l * NMOD * D:(l + 1) * NMOD * D]
        cols = lax.dynamic_slice_in_dim(dm_l, dev * ada_cols, ada_cols, axis=1)
        g_w_ada.append(_matmul(c_act, cols, "tn", f"g_ada{l}", precision=HI, tk=N_DEV * Bl))
    dm_f = dmod_all[:, L * NMOD * D:]
    cols_f = lax.dynamic_slice_in_dim(dm_f, dev * fada_cols, fada_cols, axis=1)
    g_fada_w = _matmul(c_act, cols_f, "tn", "g_fada", precision=HI, tk=N_DEV * Bl)
    g_w_ada = jnp.stack(g_w_ada)
    bsum = _bias_sum(dmod_all, "g_bias")
    g_b_ada = bsum[0, :L * NMOD * D].reshape(L, NMOD * D)
    g_fada_b = bsum[0, L * NMOD * D:]

    big_out = {}
    for k, land in zip(big, lands):
        w_, m_, v_ = wstack[k]
        big_out[k] = _adam_landed(land, w_, m_, v_, f"adam_{k}")
    R_ada = L * w_ada.shape[1]
    ada = _adam(g_w_ada.reshape(R_ada, ada_cols), w_ada.reshape(R_ada, ada_cols), m_w_ada.reshape(R_ada, ada_cols),
                v_w_ada.reshape(R_ada, ada_cols), "adam_w_ada")
    ada = [a.reshape(w_ada.shape) for a in ada]
    fada = _adam(g_fada_w, final_ada_w, m_final_ada_w, v_final_ada_w, "adam_fada")
    sm_names = ["b_ada", "norm1_w", "conv_dw_w", "conv_dw_b", "conv_ln_w", "conv_ln_b", "conv_out_norm_w", "qkv_conv_w",
                "a_log", "dt_bias", "dn_norm_w", "norm2_w", "final_ada_b", "final_norm_w"]
    sm_g = dict(b_ada=g_b_ada, norm1_w=g_norm1, conv_dw_w=g_cdw, conv_dw_b=g_cdb, conv_ln_w=g_clw, conv_ln_b=g_clb,
                conv_out_norm_w=g_con, qkv_conv_w=g_qcw, a_log=g_alog, dt_bias=g_dtb, dn_norm_w=g_dnw, norm2_w=g_norm2,
                final_ada_b=g_fada_b, final_norm_w=g_fnw)
    sm_w = dict(b_ada=(b_ada, m_b_ada, v_b_ada), norm1_w=(norm1_w, m_norm1_w, v_norm1_w),
                conv_dw_w=(conv_dw_w, m_conv_dw_w, v_conv_dw_w), conv_dw_b=(conv_dw_b, m_conv_dw_b, v_conv_dw_b),
                conv_ln_w=(conv_ln_w, m_conv_ln_w, v_conv_ln_w), conv_ln_b=(conv_ln_b, m_conv_ln_b, v_conv_ln_b),
                conv_out_norm_w=(conv_out_norm_w, m_conv_out_norm_w, v_conv_out_norm_w),
                qkv_conv_w=(qkv_conv_w, m_qkv_conv_w, v_qkv_conv_w), a_log=(a_log, m_a_log, v_a_log),
                dt_bias=(dt_bias, m_dt_bias, v_dt_bias), dn_norm_w=(dn_norm_w, m_dn_norm_w, v_dn_norm_w),
                norm2_w=(norm2_w, m_norm2_w, v_norm2_w), final_ada_b=(final_ada_b, m_final_ada_b, v_final_ada_b),
                final_norm_w=(final_norm_w, m_final_norm_w, v_final_norm_w))
    sm_shapes = [sm_w[k][0].shape for k in sm_names]
    sm = _adam(_pack([sm_g[k] for k in sm_names]), _pack([sm_w[k][0] for k in sm_names]),
               _pack([sm_w[k][1] for k in sm_names]), _pack([sm_w[k][2] for k in sm_names]), "adam_small")
    sm_d, sm_m, sm_v = (dict(zip(sm_names, _unpack(a, sm_shapes))) for a in sm)
    for k in sm_names:
        sm_g[k] = sm_g[k].reshape(sm_w[k][0].shape)

    grads = dict(w_ada=g_w_ada, final_ada_w=g_fada_w, **{k: big_out[k][0] for k in big}, **sm_g)
    deltas = dict(w_ada=ada[0], final_ada_w=fada[0], **{k: big_out[k][1] for k in big}, **sm_d)
    new_m = dict(w_ada=ada[1], final_ada_w=fada[1], **{k: big_out[k][2] for k in big}, **sm_m)
    new_v = dict(w_ada=ada[2], final_ada_w=fada[2], **{k: big_out[k][3] for k in big}, **sm_v)
    order = ["w_ada", "b_ada", "norm1_w", "w_in", "conv_dw_w", "conv_dw_b", "conv_ln_w", "conv_ln_b", "w_pw2",
             "conv_out_norm_w", "qkv_conv_w", "a_log", "dt_bias", "dn_norm_w", "w_out", "norm2_w", "w_up", "w_down",
             "final_ada_w", "final_ada_b", "final_norm_w"]
    return (loss, grad_x.reshape(Bl, S, D), *[grads[k] for k in order], *[deltas[k] for k in order],
            *[new_m[k] for k in order], *[new_v[k] for k in order])
```

```python
import functools

import numpy as np
import jax
import jax.numpy as jnp
from jax import lax
from jax.experimental import pallas as pl
from jax.experimental.pallas import tpu as pltpu

F32, BF16 = jnp.float32, jnp.bfloat16
EPS = 1e-6
CHUNK = 64
HEAD_DIM = 128
LANE = 128
SUBLANE = 8
N_DEV = 8
N_CHIP = 4
VMEM_LIMIT = 48 * 1024 * 1024
ADAM_LR, ADAM_B1, ADAM_B2, ADAM_EPS, ADAM_WD, ADAM_STEP = 0.001, 0.9, 0.999, 1e-08, 0.01, 10
MESH = pl.DeviceIdType.MESH
HI = lax.Precision.HIGHEST
NN, NT, TN = ((1,), (0,)), ((1,), (1,)), ((0,), (0,))


def _any_spec():
    return pl.BlockSpec(memory_space=pl.ANY)


def _cparams(*sem):
    return pltpu.CompilerParams(dimension_semantics=sem or None, vmem_limit_bytes=VMEM_LIMIT)


def _tile(dim, target, unit=LANE):
    best, t = None, unit
    while t <= min(dim, target):
        if dim % t == 0:
            best = t
        t += unit
    return best or dim


def _sigmoid(x):
    return jax.nn.sigmoid(x)


def _silu(x):
    return x * jax.nn.sigmoid(x)


def _softplus(x):
    return jnp.maximum(x, 0.0) + jnp.log(1.0 + jnp.exp(-jnp.abs(x)))


def _rms(x, w):
    return x * lax.rsqrt(jnp.mean(x * x, axis=-1, keepdims=True) + EPS) * w


def _matmul(a, b, mode, name, out_dtypes=(F32,), epi=None, sides=(), bias=None, precision=None,
            b_layer=None, tm=1024, tn=1024, tk=1024, comm=None, split_n=None):
    bshape = b.shape[1:] if b_layer is not None else b.shape
    if mode == "nn":
        (M, K), (K2, N) = a.shape, bshape
    elif mode == "nt":
        (M, K), (N, K2) = a.shape, bshape
    else:
        (K, M), (K2, N) = a.shape, bshape
    assert K == K2, (a.shape, b.shape, mode)
    tm, tn = _tile(M, tm), _tile(N // split_n if split_n else N, tn)
    tk = K if (K <= 2 * tk and mode != "tn") else _tile(K, tk)
    nk = K // tk
    n_per = (N // split_n) // tn if split_n else None
    assert not split_n or ((N // split_n) % tn == 0 and len(out_dtypes) == 1)
    dims = {"nn": NN, "nt": NT, "tn": TN}[mode]
    n_side, n_out, has_bias = len(sides), len(out_dtypes), bias is not None

    if mode == "tn":
        a_spec = pl.BlockSpec((tk, tm), lambda i, j, k: (k, i))
    else:
        a_spec = pl.BlockSpec((tm, tk), lambda i, j, k: (i, k))
    if mode == "nt":
        bblk, bidx = (tn, tk), (lambda i, j, k: (j, k))
    else:
        bblk, bidx = (tk, tn), (lambda i, j, k: (k, j))
    if b_layer is not None:
        b_spec = pl.BlockSpec((None,) + bblk, lambda i, j, k: (b_layer,) + bidx(i, j, k))
    else:
        b_spec = pl.BlockSpec(bblk, bidx)
    mn_spec = pl.BlockSpec((tm, tn), lambda i, j, k: (i, j))
    in_specs = [a_spec, b_spec] + [mn_spec] * n_side
    if has_bias:
        in_specs.append(pl.BlockSpec((1, tn), lambda i, j, k: (0, j)))

    n_ci, n_co = (len(comm.ins), len(comm.out_shapes)) if comm is not None else (0, 0)
    ni, nj = M // tm, N // tn
    n_in = 2 + n_side + int(has_bias)

    def body(*refs):
        a_ref, b_ref = refs[:2]
        side_refs = refs[2:2 + n_side]
        bias_ref = refs[2 + n_side] if has_bias else None
        comm_in = refs[n_in:n_in + n_ci]
        pos = n_in + n_ci
        out_refs, comm_out = refs[pos:pos + n_out], refs[pos + n_out:pos + n_out + n_co]
        acc_ref, comm_sems = refs[pos + n_out + n_co], refs[pos + n_out + n_co + 1:]
        i, j, k = pl.program_id(0), pl.program_id(1), pl.program_id(2)

        if comm is not None:
            @pl.when(jnp.logical_and(jnp.logical_and(i == 0, j == 0), k == 0))
            def _():
                comm.start(comm_in, comm_out, comm_sems)

        @pl.when(k == 0)
        def _():
            acc_ref[...] = jnp.zeros_like(acc_ref)

        av, bv = a_ref[...], b_ref[...]
        if precision is None:
            av, bv = av.astype(BF16), bv.astype(BF16)
        acc_ref[...] += lax.dot_general(av, bv, (dims, ((), ())), precision=precision,
                                        preferred_element_type=F32)

        @pl.when(k == nk - 1)
        def _():
            acc = acc_ref[...]
            if has_bias:
                acc = acc + bias_ref[...]
            outs = (acc,) if epi is None else epi(acc, *[s[...] for s in side_refs])
            for r, o in zip(out_refs, outs):
                r[...] = o.astype(r.dtype)

        if comm is not None:
            @pl.when(jnp.logical_and(jnp.logical_and(i == ni - 1, j == nj - 1), k == nk - 1))
            def _():
                comm.finish(comm_in, comm_out, comm_sems)

    sem = ("arbitrary",) * 3 if comm is not None else ("parallel", "parallel", "arbitrary")
    if split_n:
        out_specs = [pl.BlockSpec((None, tm, tn), lambda i, j, k: (j // n_per, i, j % n_per))]
        out_shape = [jax.ShapeDtypeStruct((split_n, M, N // split_n), out_dtypes[0])]
    else:
        out_specs = [mn_spec] * n_out
        out_shape = [jax.ShapeDtypeStruct((M, N), d) for d in out_dtypes]
    res = pl.pallas_call(
        body, name=name, grid=(ni, nj, nk),
        in_specs=in_specs + [_any_spec()] * n_ci, out_specs=out_specs + [_any_spec()] * n_co,
        out_shape=out_shape + (comm.out_shapes if comm is not None else []),
        input_output_aliases={n_in + ci: n_out + co for ci, co in comm.aliases.items()} if comm is not None else {},
        scratch_shapes=[pltpu.VMEM((tm, tn), F32)] + (comm.sems if comm is not None else []),
        compiler_params=_cparams(*sem),
    )(a, b, *sides, *([bias] if has_bias else []), *(comm.ins if comm is not None else []))
    main = res[0] if n_out == 1 else list(res[:n_out])
    return main if comm is None else (main, list(res[n_out:]))


def _row_specs(rows, bvecs, pvecs, tile, tpb):
    specs = [pl.BlockSpec((tile, w), lambda i, cb=cb: (i, cb)) for (_, cb, w) in rows]
    specs += [pl.BlockSpec((None, 1, v.shape[-1]), lambda i: (i // tpb, 0, 0)) for v in bvecs]
    specs += [pl.BlockSpec((1, v.shape[-1]), lambda i: (0, 0)) for v in pvecs]
    return specs


def _row_tiling(rows, bvecs, tile):
    T = rows[0][0].shape[0]
    Bl = bvecs[0].shape[0] if bvecs else 1
    tile = min(tile, T // Bl)
    assert (T // Bl) % tile == 0
    return T, tile, (T // Bl) // tile


def _rowwise(fn, rows, bvecs, pvecs, outs, name, tile=256):
    T, tile, tpb = _row_tiling(rows, bvecs, tile)
    n_in = len(rows) + len(bvecs) + len(pvecs)

    def body(*refs):
        vals = [r[...].astype(F32) for r in refs[:n_in]]
        res = fn(*vals)
        for r, o in zip(refs[n_in:], res):
            r[...] = o.astype(r.dtype)

    res = pl.pallas_call(
        body, name=name, grid=(T // tile,),
        in_specs=_row_specs(rows, bvecs, pvecs, tile, tpb),
        out_specs=[pl.BlockSpec((tile, w), lambda i: (i, 0)) for (w, _) in outs],
        out_shape=[jax.ShapeDtypeStruct((T, w), d) for (w, d) in outs],
        compiler_params=_cparams("parallel"),
    )(*[r[0] for r in rows], *bvecs, *pvecs)
    return res


def _rowwise_bwd(fn, rows, bvecs, pvecs, cots, row_grads, name, tile=128):
    T, tile, tpb = _row_tiling(rows, bvecs, tile)
    n_row, n_b, n_p, n_cot = len(rows), len(bvecs), len(pvecs), len(cots)
    n_in = n_row + n_b + n_p
    want = [k for k, d in enumerate(row_grads) if d is not None]

    def body(*refs):
        vals = [r[...].astype(F32) for r in refs[:n_in]]
        cot_vals = tuple(r[...].astype(F32) for r in refs[n_in:n_in + n_cot])
        out_refs = refs[n_in + n_cot:]
        _, vjp = jax.vjp(fn, *vals)
        g = vjp(cot_vals)
        i = pl.program_id(0)
        for r, k in zip(out_refs[:len(want)], want):
            r[...] = g[k].astype(r.dtype)
        pos = len(want)
        for q in range(n_b + n_p):
            ref, gv = out_refs[pos + q], g[n_row + q]
            first = (i % tpb == 0) if q < n_b else (i == 0)

            @pl.when(first)
            def _(ref=ref, gv=gv):
                ref[...] = gv

            @pl.when(jnp.logical_not(first))
            def _(ref=ref, gv=gv):
                ref[...] += gv

    out_specs = [pl.BlockSpec((tile, rows[k][2]), lambda i: (i, 0)) for k in want]
    out_specs += [pl.BlockSpec((None, 1, v.shape[-1]), lambda i: (i // tpb, 0, 0)) for v in bvecs]
    out_specs += [pl.BlockSpec((1, v.shape[-1]), lambda i: (0, 0)) for v in pvecs]
    out_shape = [jax.ShapeDtypeStruct((T, rows[k][2]), row_grads[k]) for k in want]
    out_shape += [jax.ShapeDtypeStruct(v.shape, F32) for v in bvecs]
    out_shape += [jax.ShapeDtypeStruct(v.shape, F32) for v in pvecs]
    res = pl.pallas_call(
        body, name=name, grid=(T // tile,),
        in_specs=_row_specs(rows, bvecs, pvecs, tile, tpb) + _row_specs(cots, [], [], tile, tpb),
        out_specs=out_specs, out_shape=out_shape,
        compiler_params=_cparams("arbitrary"),
    )(*[r[0] for r in rows], *bvecs, *pvecs, *[c[0] for c in cots])
    nw = len(want)
    return list(res[:nw]), list(res[nw:nw + n_b]), list(res[nw + n_b:])


def _f_first(x, shift, scale, w):
    return x, _rms(x, w) * (1.0 + scale) + shift


def _f_join(x1p, mp, gate2p, shift, scale, w):
    x = x1p + gate2p * mp
    return x, _rms(x, w) * (1.0 + scale) + shift


def _f_mid(x, y, gate1, shift2, scale2, w2):
    x1 = x + gate1 * y
    return x1, _rms(x1, w2) * (1.0 + scale2) + shift2


def _f_glu(val, gate):
    return (val * _sigmoid(gate),)


def _f_convmid(u1, b, lnw, lnb):
    u = u1 + b
    mu = jnp.mean(u, axis=-1, keepdims=True)
    xc = u - mu
    y = xc * lax.rsqrt(jnp.mean(xc * xc, axis=-1, keepdims=True) + EPS) * lnw + lnb
    return (_silu(y),)


def _f_rms(u, w):
    return (_rms(u, w),)


def _final_loss(x1, m, tgt, gate2, shift, scale, wf, name, tile=128):
    T, D = x1.shape
    Bl = gate2.shape[0]
    tile = min(tile, T // Bl)
    tpb = (T // Bl) // tile

    def body(x1_ref, m_ref, t_ref, g_ref, sh_ref, sc_ref, w_ref,
             dx1_ref, dm_ref, dg_ref, dsh_ref, dsc_ref, dw_ref, loss_ref):
        tg = t_ref[...]

        def lossfn(x1v, mv, g2, sh, sc, w):
            x2 = x1v + g2 * mv
            y = _rms(x2, w) * (1.0 + sc) + sh
            err = y - tg
            per_row = jnp.mean(err * err, axis=-1, keepdims=True)
            return 0.5 * jnp.sum(per_row, axis=0, keepdims=True)

        val, vjp = jax.vjp(lossfn, x1_ref[...], m_ref[...], g_ref[...], sh_ref[...], sc_ref[...], w_ref[...])
        g = vjp(jnp.ones((1, 1), F32))
        dx1_ref[...] = g[0]
        dm_ref[...] = g[1].astype(dm_ref.dtype)
        i = pl.program_id(0)
        acc = [(dg_ref, g[2], i % tpb == 0), (dsh_ref, g[3], i % tpb == 0), (dsc_ref, g[4], i % tpb == 0),
               (dw_ref, g[5], i == 0), (loss_ref, jnp.broadcast_to(val, (1, LANE)), i == 0)]
        for ref, gv, first in acc:
            @pl.when(first)
            def _(ref=ref, gv=gv):
                ref[...] = gv

            @pl.when(jnp.logical_not(first))
            def _(ref=ref, gv=gv):
                ref[...] += gv

    row = pl.BlockSpec((tile, D), lambda i: (i, 0))
    bv = pl.BlockSpec((None, 1, D), lambda i: (i // tpb, 0, 0))
    pv = pl.BlockSpec((1, D), lambda i: (0, 0))
    return pl.pallas_call(
        body, name=name, grid=(T // tile,),
        in_specs=[row, row, row, bv, bv, bv, pv],
        out_specs=[row, row, bv, bv, bv, pv, pl.BlockSpec((1, LANE), lambda i: (0, 0))],
        out_shape=[jax.ShapeDtypeStruct((T, D), F32), jax.ShapeDtypeStruct((T, D), BF16),
                   jax.ShapeDtypeStruct(gate2.shape, F32), jax.ShapeDtypeStruct(gate2.shape, F32),
                   jax.ShapeDtypeStruct(gate2.shape, F32), jax.ShapeDtypeStruct(wf.shape, F32),
                   jax.ShapeDtypeStruct((1, LANE), F32)],
        compiler_params=_cparams("arbitrary"),
    )(x1, m, tgt, gate2, shift, scale, wf)


CONV_ROWS = 128


def _conv_geometry(K):
    pad = -(-(K - 1) // SUBLANE) * SUBLANE
    return pad


def _fill_shifted(dst_ref, src_ref, shifts, length):
    for s in shifts:
        for r0 in range(0, length, CONV_ROWS):
            n = min(CONV_ROWS, length - r0)
            dst_ref[s, pl.ds(r0, n), :] = src_ref[pl.ds(r0 + s, n), :]


def _dwconv_fwd(x, xoff, w, woff, width, seq, name):
    T, K = x.shape[0], w.shape[0]
    CB = LANE
    P = _conv_geometry(K)
    nb, ncb = T // seq, width // CB
    xo, wo = xoff // CB, woff // CB
    offs = [P - (K - 1) + j for j in range(K)]
    shifts = sorted({o % SUBLANE for o in offs})
    RC = min(CONV_ROWS, seq)

    def body(x_ref, w_ref, o_ref, xpad, xs):
        xpad[pl.ds(0, P), :] = jnp.zeros((P, CB), F32)
        xpad[pl.ds(P, seq), :] = x_ref[...]
        xpad[pl.ds(P + seq, SUBLANE), :] = jnp.zeros((SUBLANE, CB), F32)
        _fill_shifted(xs, xpad, shifts, seq + P)
        wv = w_ref[...]

        def chunk(i, carry):
            r0 = pl.multiple_of(i * RC, RC)
            acc = jnp.zeros((RC, CB), F32)
            for j, o in enumerate(offs):
                acc = acc + wv[j:j + 1, :] * xs[o % SUBLANE, pl.ds(r0 + (o // SUBLANE) * SUBLANE, RC), :]
            o_ref[pl.ds(r0, RC), :] = acc
            return carry

        lax.fori_loop(0, seq // RC, chunk, 0)

    return pl.pallas_call(
        body, name=name, grid=(nb, ncb),
        in_specs=[pl.BlockSpec((seq, CB), lambda b, c: (b, xo + c)),
                  pl.BlockSpec((K, CB), lambda b, c: (0, wo + c))],
        out_specs=pl.BlockSpec((seq, CB), lambda b, c: (b, c)),
        out_shape=jax.ShapeDtypeStruct((T, width), F32),
        scratch_shapes=[pltpu.VMEM((seq + P + SUBLANE, CB), F32), pltpu.VMEM((SUBLANE, seq + P, CB), F32)],
        compiler_params=_cparams("parallel", "parallel"),
    )(x, w)


def _dwconv_bwd(x, xoff, w, woff, dy, width, seq, name, dx_dtype):
    T, K = x.shape[0], w.shape[0]
    CB = LANE
    P = _conv_geometry(K)
    nb, ncb = T // seq, width // CB
    xo, wo = xoff // CB, woff // CB
    offs = [P - (K - 1) + j for j in range(K)]
    roffs = [K - 1 - j for j in range(K)]
    xshifts = sorted({o % SUBLANE for o in offs})
    yshifts = sorted({o % SUBLANE for o in roffs})
    RC = min(CONV_ROWS, seq)

    def body(x_ref, w_ref, dy_ref, dx_ref, dw_ref, xpad, xs, ypad, ys, dwacc):
        b = pl.program_id(1)
        xpad[pl.ds(0, P), :] = jnp.zeros((P, CB), F32)
        xpad[pl.ds(P, seq), :] = x_ref[...]
        xpad[pl.ds(P + seq, SUBLANE), :] = jnp.zeros((SUBLANE, CB), F32)
        _fill_shifted(xs, xpad, xshifts, seq + P)
        ypad[pl.ds(0, seq), :] = dy_ref[...].astype(F32)
        ypad[pl.ds(seq, P + SUBLANE), :] = jnp.zeros((P + SUBLANE, CB), F32)
        _fill_shifted(ys, ypad, yshifts, seq + P)
        wv = w_ref[...]

        @pl.when(b == 0)
        def _():
            dwacc[...] = jnp.zeros_like(dwacc)

        def chunk(i, carry):
            r0 = pl.multiple_of(i * RC, RC)
            dyc = ypad[pl.ds(r0, RC), :]
            acc = jnp.zeros((RC, CB), F32)
            for j in range(K):
                o, ro = offs[j], roffs[j]
                acc = acc + wv[j:j + 1, :] * ys[ro % SUBLANE, pl.ds(r0 + (ro // SUBLANE) * SUBLANE, RC), :]
                prod = dyc * xs[o % SUBLANE, pl.ds(r0 + (o // SUBLANE) * SUBLANE, RC), :]
                part = prod[0:SUBLANE, :]
                for q in range(1, RC // SUBLANE):
                    part = part + prod[q * SUBLANE:(q + 1) * SUBLANE, :]
                dwacc[pl.ds(j * SUBLANE, SUBLANE), :] += part
            dx_ref[pl.ds(r0, RC), :] = acc.astype(dx_ref.dtype)
            return carry

        lax.fori_loop(0, seq // RC, chunk, 0)

        @pl.when(b == nb - 1)
        def _():
            for j in range(K):
                dw_ref[j:j + 1, :] = jnp.sum(dwacc[pl.ds(j * SUBLANE, SUBLANE), :], axis=0, keepdims=True)

    return pl.pallas_call(
        body, name=name, grid=(ncb, nb),
        in_specs=[pl.BlockSpec((seq, CB), lambda c, b: (b, xo + c)),
                  pl.BlockSpec((K, CB), lambda c, b: (0, wo + c)),
                  pl.BlockSpec((seq, CB), lambda c, b: (b, c))],
        out_specs=[pl.BlockSpec((seq, CB), lambda c, b: (b, c)),
                   pl.BlockSpec((K, CB), lambda c, b: (0, c))],
        out_shape=[jax.ShapeDtypeStruct((T, width), dx_dtype), jax.ShapeDtypeStruct((K, width), F32)],
        scratch_shapes=[pltpu.VMEM((seq + P + SUBLANE, CB), F32), pltpu.VMEM((SUBLANE, seq + P, CB), F32),
                        pltpu.VMEM((seq + P + SUBLANE, CB), F32), pltpu.VMEM((SUBLANE, seq + P, CB), F32),
                        pltpu.VMEM((K * SUBLANE, CB), F32)],
        compiler_params=_cparams("parallel", "arbitrary"),
    )(x, w, dy)


def _dot_raw(a, b, dims, mode):
    o = a.ndim - 2
    contract = {"nn": ((o + 1,), (o,)), "nt": ((o + 1,), (o + 1,)), "tn": ((o,), (o,))}[dims]
    batch = (tuple(range(o)), tuple(range(o)))

    def d(u, v):
        return lax.dot_general(u, v, (contract, batch), preferred_element_type=F32)

    ah, bh = a.astype(BF16), b.astype(BF16)
    if mode == "bf16":
        return d(ah, bh)
    al, bl = (a - ah.astype(F32)).astype(BF16), (b - bh.astype(F32)).astype(BF16)
    return d(ah, bh) + (d(ah, bl) + d(al, bh))


@functools.partial(jax.custom_vjp, nondiff_argnums=(2, 3))
def _dot_vjp(a, b, dims, mode):
    return _dot_raw(a, b, dims, mode)


def _dot_vjp_fwd(a, b, dims, mode):
    return _dot_raw(a, b, dims, mode), (a, b)


def _dot_vjp_bwd(dims, mode, res, g):
    a, b = res
    if dims == "nn":
        return _dot_raw(g, b, "nt", mode), _dot_raw(a, g, "tn", mode)
    if dims == "nt":
        return _dot_raw(g, b, "nn", mode), _dot_raw(g, a, "tn", mode)
    return _dot_raw(b, g, "nt", mode), _dot_raw(a, g, "nn", mode)


_dot_vjp.defvjp(_dot_vjp_fwd, _dot_vjp_bwd)


def _tri_inv_raw(L):
    C = L.shape[-1]
    shp = (1,) * (L.ndim - 2) + (C, C)
    eye = lax.broadcasted_iota(jnp.int32, shp, L.ndim - 2) == lax.broadcasted_iota(jnp.int32, shp, L.ndim - 1)
    Tm = jnp.where(eye, 1.0, 0.0) - L
    Mp = _dot_raw(L, L, "nn", "x3")
    n_sq = int(np.ceil(np.log2(C))) - 1
    for it in range(n_sq):
        Tm = Tm + _dot_raw(Tm, Mp, "nn", "x3")
        if it + 1 < n_sq:
            Mp = _dot_raw(Mp, Mp, "nn", "x3")
    return Tm


@jax.custom_vjp
def _tri_inv_vjp(L):
    return _tri_inv_raw(L)


def _tri_inv_fwd(L):
    Tm = _tri_inv_raw(L)
    return Tm, Tm


def _tri_inv_bwd(Tm, g):
    return (-_dot_raw(_dot_raw(Tm, g, "tn", "x3"), Tm, "nt", "x3"),)


_tri_inv_vjp.defvjp(_tri_inv_fwd, _tri_inv_bwd)


def _delta_chunk(S, qr, kr, vr, z, lg, a_vec, dt_vec, nw, h0, nheads, diff=False):
    _dot = _dot_vjp if diff else _dot_raw
    _tri_inv = _tri_inv_vjp if diff else _tri_inv_raw
    HB, C, D = qr.shape
    lane = lax.broadcasted_iota(jnp.int32, (HB, 1, LANE), 2)
    head = lax.broadcasted_iota(jnp.int32, (HB, 1, LANE), 0) + h0
    sel_b, sel_a = lane == head, lane == head + nheads

    def pick(sel, mat):
        return jnp.sum(jnp.where(sel, mat[None], 0.0), axis=-1, keepdims=True)

    b_logit, a_logit = pick(sel_b, lg), pick(sel_a, lg)
    a_h, dt_h = pick(sel_b, a_vec), pick(sel_b, dt_vec)

    q, k, v = _silu(qr), _silu(kr), _silu(vr)
    q = q * lax.rsqrt(jnp.sum(q * q, axis=-1, keepdims=True) + EPS) * (D ** -0.5)
    k = k * lax.rsqrt(jnp.sum(k * k, axis=-1, keepdims=True) + EPS)
    beta = _sigmoid(b_logit)
    g = -jnp.exp(a_h) * _softplus(a_logit + dt_h)

    row = lax.broadcasted_iota(jnp.int32, (1, C, C), 1)
    col = lax.broadcasted_iota(jnp.int32, (1, C, C), 2)
    causal, strict, eye = row >= col, row > col, row == col
    tril = jnp.broadcast_to(causal.astype(F32), (HB, C, C))
    gc_d = _dot(tril, jnp.broadcast_to(g, (HB, C, D)), "nn", "x3")
    gi = _dot(tril, jnp.broadcast_to(g, (HB, C, C)), "nn", "x3")
    gj = _dot(jnp.ones((HB, C, C), F32), jnp.where(eye, gi, 0.0), "nn", "x3")
    gtot = jnp.sum(g, axis=1, keepdims=True)
    decay = jnp.exp(jnp.where(causal, gi - gj, -1e30))
    e_gc = jnp.exp(gc_d)

    kb, vb = k * beta, v * beta
    L = jnp.where(strict, _dot(kb, k, "nt", "bf16") * decay, 0.0)
    Tm = _tri_inv(L)
    u = _dot(Tm, vb, "nn", "x3")
    w = _dot(Tm, kb * e_gc, "nn", "x3")
    attn = jnp.where(causal, _dot(q, k, "nt", "bf16") * decay, 0.0)
    v_new = u - _dot(w, S, "nn", "bf16")
    o = _dot(q * e_gc, S, "nn", "bf16") + _dot(attn, v_new, "nn", "bf16")
    S_new = S * jnp.exp(gtot) + _dot(k * jnp.exp(gtot - gc_d), v_new, "tn", "bf16")
    y = _rms(o, nw) * _silu(z)
    return y, S_new


def _delta_specs(DN, CC, HP, NC, rev):
    W = HP * HEAD_DIM
    nb = (lambda n: NC - 1 - n) if rev else (lambda n: n)
    qkv = [pl.BlockSpec((CHUNK, W), lambda b, g, n, p=p: (b * NC + nb(n), p * (DN // W) + g)) for p in range(3)]
    z = pl.BlockSpec((CHUNK, W), lambda b, g, n: (b * NC + nb(n), (2 * CC + 3 * DN) // W + g))
    lg = pl.BlockSpec((CHUNK, LANE), lambda b, g, n: (b * NC + nb(n), (2 * CC + 4 * DN) // LANE))
    pv = pl.BlockSpec((1, LANE), lambda b, g, n: (0, 0))
    return qkv, z, lg, pv, nb


def _delta_fwd(qkv_c, proj, a_vec, dt_vec, nw, DN, CC, seq, name, HP, comm=None):
    T = qkv_c.shape[0]
    H, NC, Bl = DN // HEAD_DIM, seq // CHUNK, T // seq
    W, G = HP * HEAD_DIM, DN // (HP * HEAD_DIM)
    qkv, zs, lgs, pv, nb = _delta_specs(DN, CC, HP, NC, False)
    n_ci, n_co = (len(comm.ins), len(comm.out_shapes)) if comm is not None else (0, 0)
    n_in, n_out = 8, 2

    def body(*refs):
        q_ref, k_ref, v_ref, z_ref, lg_ref, a_ref, dt_ref, nw_ref = refs[:n_in]
        comm_in = refs[n_in:n_in + n_ci]
        pos = n_in + n_ci
        y_ref, ssave_ref = refs[pos:pos + n_out]
        comm_out = refs[pos + n_out:pos + n_out + n_co]
        S_ref, comm_sems = refs[pos + n_out + n_co], refs[pos + n_out + n_co + 1:]
        b, grp, n = pl.program_id(0), pl.program_id(1), pl.program_id(2)

        if comm is not None:
            @pl.when(jnp.logical_and(jnp.logical_and(b == 0, grp == 0), n == 0))
            def _():
                comm.start(comm_in, comm_out, comm_sems)

        @pl.when(n == 0)
        def _():
            S_ref[...] = jnp.zeros_like(S_ref)

        lg, av, dv, nv = lg_ref[...], a_ref[...], dt_ref[...], nw_ref[...]
        sls = [slice(hh * HEAD_DIM, (hh + 1) * HEAD_DIM) for hh in range(HP)]
        heads = lambda ref: jnp.stack([ref[:, sl].astype(F32) for sl in sls])
        S = S_ref[...]
        ssave_ref[...] = S
        y, S_new = _delta_chunk(S, heads(q_ref), heads(k_ref), heads(v_ref), heads(z_ref),
                                lg, av, dv, nv, grp * HP, H)
        for hh, sl in enumerate(sls):
            y_ref[:, sl] = y[hh].astype(y_ref.dtype)
        S_ref[...] = S_new

        if comm is not None:
            @pl.when(jnp.logical_and(jnp.logical_and(b == Bl - 1, grp == G - 1), n == NC - 1))
            def _():
                comm.finish(comm_in, comm_out, comm_sems)

    sem = ("arbitrary",) * 3 if comm is not None else ("parallel", "parallel", "arbitrary")
    res = pl.pallas_call(
        body, name=name, grid=(Bl, G, NC),
        in_specs=[*qkv, zs, lgs, pv, pv, pv] + [_any_spec()] * n_ci,
        out_specs=[pl.BlockSpec((CHUNK, W), lambda b, g, n: (b * NC + n, g)),
                   pl.BlockSpec((None, HP, None, HEAD_DIM, HEAD_DIM), lambda b, g, n: (b, g, n, 0, 0))]
                  + [_any_spec()] * n_co,
        out_shape=[jax.ShapeDtypeStruct((T, DN), BF16), jax.ShapeDtypeStruct((Bl, H, NC, HEAD_DIM, HEAD_DIM), F32)]
                  + (comm.out_shapes if comm is not None else []),
        input_output_aliases={n_in + ci: n_out + co for ci, co in comm.aliases.items()} if comm is not None else {},
        scratch_shapes=[pltpu.VMEM((HP, HEAD_DIM, HEAD_DIM), F32)] + (comm.sems if comm is not None else []),
        compiler_params=_cparams(*sem),
    )(qkv_c, qkv_c, qkv_c, proj, proj, a_vec, dt_vec, nw, *(comm.ins if comm is not None else []))
    return res[0], res[1], list(res[n_out:])


def _delta_bwd(qkv_c, proj, a_vec, dt_vec, nw, ssave, dy, dyoff, DN, CC, seq, name, HP, comm=None):
    T = qkv_c.shape[0]
    H, NC, Bl = DN // HEAD_DIM, seq // CHUNK, T // seq
    W, G = HP * HEAD_DIM, DN // (HP * HEAD_DIM)
    qkv, zs, lgs, pv, nb = _delta_specs(DN, CC, HP, NC, True)
    n_ci, n_co = (len(comm.ins), len(comm.out_shapes)) if comm is not None else (0, 0)
    n_in, n_out = 10, 8

    def body(*refs):
        q_ref, k_ref, v_ref, z_ref, lg_ref, a_ref, dt_ref, nw_ref, ss_ref, dy_ref = refs[:n_in]
        comm_in = refs[n_in:n_in + n_ci]
        pos = n_in + n_ci
        dq_ref, dk_ref, dv_ref, dz_ref, dlg_ref, da_ref, ddt_ref, dnw_ref = refs[pos:pos + n_out]
        comm_out = refs[pos + n_out:pos + n_out + n_co]
        dS_ref, comm_sems = refs[pos + n_out + n_co], refs[pos + n_out + n_co + 1:]
        b, grp, n = pl.program_id(0), pl.program_id(1), pl.program_id(2)

        if comm is not None:
            @pl.when(jnp.logical_and(jnp.logical_and(b == 0, grp == 0), n == 0))
            def _():
                comm.start(comm_in, comm_out, comm_sems)

        @pl.when(n == 0)
        def _():
            dS_ref[...] = jnp.zeros_like(dS_ref)

        lg, av, dv, nv = lg_ref[...], a_ref[...], dt_ref[...], nw_ref[...]
        sls = [slice(hh * HEAD_DIM, (hh + 1) * HEAD_DIM) for hh in range(HP)]
        heads = lambda ref: jnp.stack([ref[:, sl].astype(F32) for sl in sls])
        fn = functools.partial(_delta_chunk, h0=grp * HP, nheads=H, diff=True)
        _, vjp = jax.vjp(fn, ss_ref[...], heads(q_ref), heads(k_ref), heads(v_ref), heads(z_ref), lg, av, dv, nv)
        gS, gq, gk, gv, gz, glg, ga, gdt, gnw = vjp((heads(dy_ref), dS_ref[...]))
        dS_ref[...] = gS
        for hh, sl in enumerate(sls):
            dq_ref[:, sl] = gq[hh].astype(dq_ref.dtype)
            dk_ref[:, sl] = gk[hh].astype(dk_ref.dtype)
            dv_ref[:, sl] = gv[hh].astype(dv_ref.dtype)
            dz_ref[:, sl] = gz[hh].astype(dz_ref.dtype)
        dlg_ref[...] = glg
        dpar = [ga, gdt, gnw]
        first = jnp.logical_and(jnp.logical_and(b == 0, grp == 0), n == 0)
        for ref, gv_ in zip((da_ref, ddt_ref, dnw_ref), dpar):
            @pl.when(first)
            def _(ref=ref, gv_=gv_):
                ref[...] = gv_

            @pl.when(jnp.logical_not(first))
            def _(ref=ref, gv_=gv_):
                ref[...] += gv_

        if comm is not None:
            @pl.when(jnp.logical_and(jnp.logical_and(b == Bl - 1, grp == G - 1), n == NC - 1))
            def _():
                comm.finish(comm_in, comm_out, comm_sems)

    rowb = pl.BlockSpec((CHUNK, W), lambda b, g, n: (b * NC + nb(n), g))
    res = pl.pallas_call(
        body, name=name, grid=(Bl, G, NC),
        in_specs=[*qkv, zs, lgs, pv, pv, pv,
                  pl.BlockSpec((None, HP, None, HEAD_DIM, HEAD_DIM), lambda b, g, n: (b, g, nb(n), 0, 0)),
                  pl.BlockSpec((CHUNK, W), lambda b, g, n: (b * NC + nb(n), dyoff // W + g))] + [_any_spec()] * n_ci,
        out_specs=[rowb, rowb, rowb, rowb,
                   pl.BlockSpec((None, CHUNK, LANE), lambda b, g, n: (g, b * NC + nb(n), 0)),
                   pv, pv, pv] + [_any_spec()] * n_co,
        out_shape=[jax.ShapeDtypeStruct((T, DN), F32)] * 3 + [jax.ShapeDtypeStruct((T, DN), BF16),
                   jax.ShapeDtypeStruct((G, T, LANE), F32)] + [jax.ShapeDtypeStruct((1, LANE), F32)] * 3
                  + (comm.out_shapes if comm is not None else []),
        input_output_aliases={n_in + ci: n_out + co for ci, co in comm.aliases.items()} if comm is not None else {},
        scratch_shapes=[pltpu.VMEM((HP, HEAD_DIM, HEAD_DIM), F32)] + (comm.sems if comm is not None else []),
        compiler_params=_cparams("arbitrary", "arbitrary", "arbitrary"),
    )(qkv_c, qkv_c, qkv_c, proj, proj, a_vec, dt_vec, nw, ssave, dy, *(comm.ins if comm is not None else []))
    return list(res[:n_out]), list(res[n_out:])


def _place():
    x, y, c = lax.axis_index("x"), lax.axis_index("y"), lax.axis_index("c")
    chips = [(1 - x, y), (x, 1 - y), (1 - x, 1 - y)]
    return x, y, c, chips


class _Comm:
    def __init__(self, ins, out_shapes, aliases, sems, start, finish):
        self.ins, self.out_shapes, self.aliases, self.sems = list(ins), list(out_shapes), dict(aliases), list(sems)
        self.start, self.finish = start, finish


def _run_comm(comm, name):
    ni, no = len(comm.ins), len(comm.out_shapes)

    def body(*refs):
        ins, outs, sems = refs[:ni], refs[ni:ni + no], refs[ni + no:]
        comm.start(ins, outs, sems)
        comm.finish(ins, outs, sems)

    return pl.pallas_call(
        body, name=name,
        in_specs=[_any_spec()] * ni, out_specs=[_any_spec()] * no, out_shape=comm.out_shapes,
        input_output_aliases=comm.aliases, scratch_shapes=comm.sems,
    )(*comm.ins)


def _ag_comm(shards):
    n = len(shards)

    def parts(ins, outs, sems):
        send_sems, recv_sems, local_sems = sems
        x, y, c, chips = _place()
        me, sibling = (x, y, c), (x, y, 1 - c)

        def copy(a, k, block, to, src=None):
            dst = outs[a].at[4 * block[0] + 2 * block[1] + block[2]]
            return pltpu.make_async_remote_copy(
                src_ref=dst if src is None else src, dst_ref=dst,
                send_sem=send_sems.at[a * 7 + k], recv_sem=recv_sems.at[a * 7 + k],
                device_id=to, device_id_type=MESH)

        mine = [pltpu.make_async_copy(ins[a], outs[a].at[4 * x + 2 * y + c], local_sems.at[a]) for a in range(n)]
        first = []
        for a in range(n):
            first.append(copy(a, 0, me, sibling, src=ins[a]))
            first += [copy(a, 1 + j, me, (*chip, c), src=ins[a]) for j, chip in enumerate(chips)]
        return c, chips, me, sibling, copy, mine, first

    def start(ins, outs, sems):
        _, _, _, _, _, mine, first = parts(ins, outs, sems)
        for cp in mine + first:
            cp.start()

    def finish(ins, outs, sems):
        c, chips, me, sibling, copy, mine, first = parts(ins, outs, sems)
        passed = []
        for j, chip in enumerate(chips):
            for a in range(n):
                copy(a, 1 + j, (*chip, c), me).wait_recv()
                fwd = copy(a, 4 + j, (*chip, c), sibling)
                fwd.start()
                passed.append(fwd)
        for a in range(n):
            copy(a, 0, sibling, me).wait_recv()
            for j, chip in enumerate(chips):
                copy(a, 4 + j, (*chip, 1 - c), me).wait_recv()
        for cp in first + passed:
            cp.wait_send()
        for cp in mine:
            cp.wait()

    return _Comm(shards, [jax.ShapeDtypeStruct((N_DEV,) + s.shape, s.dtype) for s in shards], {},
                 [pltpu.SemaphoreType.DMA((7 * n,)), pltpu.SemaphoreType.DMA((7 * n,)), pltpu.SemaphoreType.DMA((n,))],
                 start, finish)


def _all_gather(shards, name):
    return _run_comm(_ag_comm(shards), name)


def _sib_comm(grads):
    n = len(grads)

    def copies(ins, outs, sems):
        send_sems, recv_sems = sems
        x, y, c, _ = _place()
        return [pltpu.make_async_remote_copy(
            src_ref=ins[a].at[:, 1 - c], dst_ref=outs[a],
            send_sem=send_sems.at[a], recv_sem=recv_sems.at[a],
            device_id=(x, y, 1 - c), device_id_type=MESH) for a in range(n)]

    def start(ins, outs, sems):
        for cp in copies(ins, outs, sems):
            cp.start()

    def finish(ins, outs, sems):
        for cp in copies(ins, outs, sems):
            cp.wait()

    return _Comm(grads, [jax.ShapeDtypeStruct((N_CHIP,) + g.shape[2:], g.dtype) for g in grads], {},
                 [pltpu.SemaphoreType.DMA((n,)), pltpu.SemaphoreType.DMA((n,))], start, finish)


def _chip_comm(parts, lands, layer):
    n = len(parts)

    def copies(refs_in, outs, sems):
        ins = refs_in[:n]
        send_sems, recv_sems, local_sems = sems
        x, y, c, chips = _place()
        myq = 2 * x + y
        mine = [pltpu.make_async_copy(ins[a].at[myq], outs[a].at[layer, myq], local_sems.at[a]) for a in range(n)]
        sends, recvs = [], []
        for a in range(n):
            for j, (cx, cy) in enumerate(chips):
                kw = dict(send_sem=send_sems.at[a * 3 + j], recv_sem=recv_sems.at[a * 3 + j],
                          device_id=(cx, cy, c), device_id_type=MESH)
                sends.append(pltpu.make_async_remote_copy(
                    src_ref=ins[a].at[2 * cx + cy], dst_ref=outs[a].at[layer, myq], **kw))
                recvs.append(pltpu.make_async_remote_copy(
                    src_ref=ins[a].at[2 * cx + cy], dst_ref=outs[a].at[layer, 2 * cx + cy], **kw))
        return mine, sends, recvs

    def start(ins, outs, sems):
        mine, sends, _ = copies(ins, outs, sems)
        for cp in mine + sends:
            cp.start()

    def finish(ins, outs, sems):
        mine, sends, recvs = copies(ins, outs, sems)
        for cp in recvs:
            cp.wait_recv()
        for cp in sends:
            cp.wait_send()
        for cp in mine:
            cp.wait()

    return _Comm(list(parts) + list(lands), [jax.ShapeDtypeStruct(l.shape, l.dtype) for l in lands],
                 {n + a: a for a in range(n)},
                 [pltpu.SemaphoreType.DMA((3 * n,)), pltpu.SemaphoreType.DMA((3 * n,)), pltpu.SemaphoreType.DMA((n,))],
                 start, finish)


def _pair_sum(g, land, name, tr=256):
    _, _, R, C = g.shape
    tr = _tile(R, tr, SUBLANE)
    core = lax.axis_index("c").astype(jnp.int32).reshape(1)

    def body(c_ref, g_ref, l_ref, o_ref):
        o_ref[...] = (g_ref[...] + l_ref[...]).astype(o_ref.dtype)

    return pl.pallas_call(
        body, name=name,
        grid_spec=pltpu.PrefetchScalarGridSpec(
            num_scalar_prefetch=1, grid=(N_CHIP, R // tr),
            in_specs=[pl.BlockSpec((None, None, tr, C), lambda q, i, cr: (q, cr[0], i, 0)),
                      pl.BlockSpec((None, tr, C), lambda q, i, cr: (q, i, 0))],
            out_specs=pl.BlockSpec((None, tr, C), lambda q, i, cr: (q, i, 0))),
        out_shape=jax.ShapeDtypeStruct((N_CHIP, R, C), BF16),
        compiler_params=_cparams("parallel", "parallel"),
    )(core, g, land)


def _adam_math(w, g, m, v):
    m = ADAM_B1 * m + (1.0 - ADAM_B1) * g
    v = ADAM_B2 * v + (1.0 - ADAM_B2) * jnp.square(g)
    m_hat = m / (1.0 - ADAM_B1 ** ADAM_STEP)
    v_hat = v / (1.0 - ADAM_B2 ** ADAM_STEP)
    delta = -ADAM_LR * (m_hat / (jnp.sqrt(v_hat) + ADAM_EPS) + ADAM_WD * w)
    return delta, m, v


def _adam_landed(land, w, m, v, name, tr=256):
    L, _, R, C = land.shape
    tr = _tile(R, tr, SUBLANE)

    def body(l_ref, w_ref, m_ref, v_ref, g_ref, d_ref, nm_ref, nv_ref):
        g = l_ref[0].astype(F32)
        for q in range(1, N_CHIP):
            g = g + l_ref[q].astype(F32)
        d, nm, nv = _adam_math(w_ref[...], g, m_ref[...], v_ref[...])
        g_ref[...], d_ref[...], nm_ref[...], nv_ref[...] = g, d, nm, nv

    blk = pl.BlockSpec((None, tr, C), lambda l, i: (l, i, 0))
    return pl.pallas_call(
        body, name=name, grid=(L, R // tr),
        in_specs=[pl.BlockSpec((None, N_CHIP, tr, C), lambda l, i: (l, 0, i, 0)), blk, blk, blk],
        out_specs=[blk] * 4, out_shape=[jax.ShapeDtypeStruct((L, R, C), F32)] * 4,
        compiler_params=_cparams("parallel", "parallel"),
    )(land, w, m, v)


def _adam(g, w, m, v, name, tr=256):
    L, R, C = w.shape
    tr = _tile(R, tr, SUBLANE)

    def body(g_ref, w_ref, m_ref, v_ref, d_ref, nm_ref, nv_ref):
        d_ref[...], nm_ref[...], nv_ref[...] = _adam_math(w_ref[...], g_ref[...], m_ref[...], v_ref[...])

    blk = pl.BlockSpec((None, tr, C), lambda l, i: (l, i, 0))
    return pl.pallas_call(
        body, name=name, grid=(L, R // tr),
        in_specs=[blk] * 4, out_specs=[blk] * 3, out_shape=[jax.ShapeDtypeStruct((L, R, C), F32)] * 3,
        compiler_params=_cparams("parallel", "parallel"),
    )(g, w, m, v)


def _bias_sum(a, name, tn=2048):
    B, N = a.shape
    tn = _tile(N, tn)

    def body(a_ref, o_ref):
        o_ref[...] = jnp.sum(a_ref[...], axis=0, keepdims=True)

    return pl.pallas_call(
        body, name=name, grid=(N // tn,),
        in_specs=[pl.BlockSpec((B, tn), lambda j: (0, j))],
        out_specs=pl.BlockSpec((1, tn), lambda j: (0, j)),
        out_shape=jax.ShapeDtypeStruct((1, N), F32),
        compiler_params=_cparams("parallel"),
    )(a)


def _sum_devices(stack, name):
    _, R, C = stack.shape

    def body(s_ref, o_ref):
        acc = s_ref[0]
        for d in range(1, N_DEV):
            acc = acc + s_ref[d]
        o_ref[...] = acc

    return pl.pallas_call(
        body, name=name, grid=(1,),
        in_specs=[pl.BlockSpec((N_DEV, R, C), lambda i: (0, 0, 0))],
        out_specs=pl.BlockSpec((R, C), lambda i: (0, 0)),
        out_shape=jax.ShapeDtypeStruct((R, C), F32),
    )(stack)


def _pack(arrs):
    flat = jnp.concatenate([a.reshape(-1) for a in arrs])
    n = flat.shape[0]
    rows = -(-n // (LANE * SUBLANE)) * SUBLANE
    flat = jnp.pad(flat, (0, rows * LANE - n))
    return flat.reshape(rows, LANE)


def _unpack(packed, shapes):
    flat = packed.reshape(-1)
    out, pos = [], 0
    for s in shapes:
        n = int(np.prod(s))
        out.append(flat[pos:pos + n].reshape(s))
        pos += n
    return out


def _cols_to_shards(g, n_cols):
    R = g.shape[0]
    return g[:, :n_cols].reshape(R, N_DEV, n_cols // N_DEV).transpose(1, 0, 2).reshape(N_CHIP, 2, R, n_cols // N_DEV)


def _rows_to_shards(g):
    R, C = g.shape
    return g.reshape(N_CHIP, 2, R // N_DEV, C)


def _shards_to_cols(w8, pad_to=None):
    _, R, Cs = w8.shape
    w = w8.transpose(1, 0, 2).reshape(R, N_DEV * Cs)
    if pad_to is not None and pad_to > N_DEV * Cs:
        w = jnp.pad(w, ((0, 0), (0, pad_to - N_DEV * Cs)))
    return w


def kernel(x, c, w_ada, b_ada, norm1_w, w_in, conv_dw_w, conv_dw_b, conv_ln_w, conv_ln_b, w_pw2, conv_out_norm_w, qkv_conv_w, a_log, dt_bias, dn_norm_w, w_out, norm2_w, w_up, w_down, final_ada_w, final_ada_b, final_norm_w, loss_target, m_w_ada, m_b_ada, m_norm1_w, m_w_in, m_conv_dw_w, m_conv_dw_b, m_conv_ln_w, m_conv_ln_b, m_w_pw2, m_conv_out_norm_w, m_qkv_conv_w, m_a_log, m_dt_bias, m_dn_norm_w, m_w_out, m_norm2_w, m_w_up, m_w_down, m_final_ada_w, m_final_ada_b, m_final_norm_w, v_w_ada, v_b_ada, v_norm1_w, v_w_in, v_conv_dw_w, v_conv_dw_b, v_conv_ln_w, v_conv_ln_b, v_w_pw2, v_conv_out_norm_w, v_qkv_conv_w, v_a_log, v_dt_bias, v_dn_norm_w, v_w_out, v_norm2_w, v_w_up, v_w_down, v_final_ada_w, v_final_ada_b, v_final_norm_w):
    Bl, S, D = x.shape
    T = Bl * S
    L = w_ada.shape[0]
    CC = conv_ln_w.shape[1]
    DN = qkv_conv_w.shape[2] * N_DEV // 3
    H = a_log.shape[1]
    IN_COLS = w_in.shape[2] * N_DEV
    IN_PAD = -(-IN_COLS // LANE) * LANE
    DFF = w_up.shape[2] * N_DEV
    NMOD = w_ada.shape[2] * N_DEV // D
    ada_cols = w_ada.shape[2]
    fada_cols = final_ada_w.shape[1]
    HP = min(8, H)
    assert DN == H * HEAD_DIM and NMOD == 6 and S % CHUNK == 0

    dev = 4 * lax.axis_index("x") + 2 * lax.axis_index("y") + lax.axis_index("c")
    xf = x.reshape(T, D)
    tgt = loss_target.reshape(T, D)

    c8, cdw8, qcw8 = _all_gather([c, conv_dw_w, qkv_conv_w], "ag_small")
    c_all = c8.reshape(N_DEV * Bl, D)
    conv_w_full = cdw8.transpose(1, 2, 0, 3).reshape(L, conv_dw_w.shape[1], CC)
    qkv_w_full = qcw8.transpose(1, 2, 0, 3).reshape(L, qkv_conv_w.shape[1], 3 * DN)
    (c_act,) = _rowwise(lambda cv: (_silu(cv),), [(c_all, 0, D)], [], [], [(D, F32)], "c_act", tile=N_DEV * Bl)
    mod_cols = []
    for l in range(L):
        bias = lax.dynamic_slice_in_dim(b_ada[l], dev * ada_cols, ada_cols).reshape(1, ada_cols)
        mod_cols.append(_matmul(c_act, w_ada, "nn", f"mod{l}", bias=bias, b_layer=l, tk=2048))
    bias_f = lax.dynamic_slice_in_dim(final_ada_b, dev * fada_cols, fada_cols).reshape(1, fada_cols)
    mod_cols.append(_matmul(c_act, final_ada_w, "nn", "modf", bias=bias_f, tk=2048))
    (mod8,) = _all_gather([jnp.concatenate(mod_cols, axis=1)], "ag_mod")
    mod8 = lax.dynamic_slice_in_dim(mod8, dev * Bl, Bl, axis=1)
    mod_l = mod8[:, :, :L * ada_cols].reshape(N_DEV, Bl, L, ada_cols).transpose(1, 2, 0, 3).reshape(Bl, L, NMOD, 1, D)
    mod_f = mod8[:, :, L * ada_cols:].transpose(1, 0, 2).reshape(Bl, 2, 1, D)
    shift_f, scale_f = mod_f[:, 0], mod_f[:, 1]

    def modv(l, k):
        return mod_l[:, l, k]

    pv = lambda a: a.reshape(1, -1)
    padl = lambda a: jnp.pad(a, (0, LANE - a.shape[0])).reshape(1, LANE)

    saved = []
    wts = []
    x1p = mp = None
    big = ("w_in", "w_pw2", "w_out", "w_up", "w_down")
    N_EARLY = 3
    wsrc = dict(w_in=w_in, w_pw2=w_pw2, w_out=w_out, w_up=w_up, w_down=w_down)
    shard16 = lambda k, l: wsrc[k][l].astype(BF16)
    layout = dict(w_in=lambda g: _shards_to_cols(g, IN_PAD), w_pw2=lambda g: g.reshape(CC, CC),
                  w_out=lambda g: g.reshape(D, D), w_up=_shards_to_cols, w_down=lambda g: g.reshape(DFF, D))
    g8 = dict(zip(big[:N_EARLY], _all_gather([shard16(k, 0) for k in big[:N_EARLY]], "ag_w0")))
    for l in range(L):
        W = {k: layout[k](g) for k, g in g8.items()}
        wts.append(W)
        sv = {}
        g8 = {}

        def _matmul_ag(*args, ag, l=l, g8=g8, **kw):
            if l + 1 == L:
                return _matmul(*args, **kw)
            main, outs = _matmul(*args, comm=_ag_comm([shard16(k, l + 1) for k in ag]), **kw)
            g8.update(zip(ag, outs))
            return main

        if l == 0:
            (h,) = _rowwise(lambda *v: (_f_first(*v)[1],), [(xf, 0, D)], [modv(l, 0), modv(l, 1)], [pv(norm1_w[l])],
                            [(D, BF16)], f"pre{l}")
            xl = xf
        else:
            xl, h = _rowwise(_f_join, [(x1p, 0, D), (mp, 0, D)], [modv(l - 1, 5), modv(l, 0), modv(l, 1)],
                             [pv(norm1_w[l])], [(D, F32), (D, BF16)], f"pre{l}")
        proj = _matmul_ag(h, W["w_in"], "nn", f"proj{l}", ag=("w_down",))
        (u0,) = _rowwise(_f_glu, [(proj, 0, CC), (proj, 1, CC)], [], [], [(CC, F32)], f"glu{l}")
        u1 = _dwconv_fwd(u0, 0, conv_w_full[l], 0, CC, S, f"conv{l}")
        cpv = [pv(conv_dw_b[l]), pv(conv_ln_w[l]), pv(conv_ln_b[l])]
        (u2,) = _rowwise(_f_convmid, [(u1, 0, CC)], [], cpv, [(CC, BF16)], f"convmid{l}")
        u3 = _matmul(u2, W["w_pw2"], "nn", f"pw2{l}")
        (y_conv,) = _rowwise(_f_rms, [(u3, 0, CC)], [], [pv(conv_out_norm_w[l])], [(CC, BF16)], f"convout{l}")
        qkv_c = _dwconv_fwd(proj, 2 * CC, qkv_w_full[l], 0, 3 * DN, S, f"qkvconv{l}")
        dvec = [padl(a_log[l]), padl(dt_bias[l]), pv(dn_norm_w[l])]
        y_dn, ssave, late0 = _delta_fwd(
            qkv_c, proj, *dvec, DN, CC, S, f"delta{l}", HP,
            comm=_ag_comm([shard16(k, 0) for k in big[N_EARLY:]]) if l == 0 else None)
        if l == 0:
            W.update({k: layout[k](g) for k, g in zip(big[N_EARLY:], late0)})
        ycat = jnp.concatenate([y_conv, y_dn], axis=1)
        y = _matmul_ag(ycat, W["w_out"], "nn", f"out{l}", ag=("w_pw2", "w_out"))
        x1, h2 = _rowwise(_f_mid, [(xl, 0, D), (y, 0, D)], [modv(l, 2), modv(l, 3), modv(l, 4)], [pv(norm2_w[l])],
                          [(D, F32), (D, BF16)], f"mid{l}")
        a_act, r_act = _matmul_ag(h2, W["w_up"], "nn", f"up{l}", out_dtypes=(BF16, BF16), ag=("w_up",),
                                  epi=lambda acc: (acc, jnp.square(jnp.maximum(acc, 0.0))))
        m = _matmul_ag(r_act, W["w_down"], "nn", f"down{l}", ag=("w_in",))
        sv.update(xl=xl, h=h, proj=proj, u0=u0, u1=u1, u2=u2, u3=u3, qkv_c=qkv_c, ssave=ssave, ycat=ycat, y=y,
                  x1=x1, h2=h2, a=a_act, r=r_act, m=m, x1p=x1p, mp=mp, cpv=cpv, dvec=dvec)
        saved.append(sv)
        x1p, mp = x1, m

    dx1, dm, dgate2, dshift_f, dscale_f, dwf, loss_part = _final_loss(
        x1p, mp, tgt, modv(L - 1, 5), shift_f, scale_f, pv(final_norm_w), "final")

    wstack = dict(w_in=(w_in, m_w_in, v_w_in), w_pw2=(w_pw2, m_w_pw2, v_w_pw2), w_out=(w_out, m_w_out, v_w_out),
                  w_up=(w_up, m_w_up, v_w_up), w_down=(w_down, m_w_down, v_w_down))
    lands = [lax.empty((L, N_CHIP) + wstack[k][0].shape[1:], BF16) for k in big]
    dmod = [[None] * NMOD for _ in range(L)]
    small = {k: [None] * L for k in ("norm1_w", "conv_dw_w", "conv_dw_b", "conv_ln_w", "conv_ln_b", "conv_out_norm_w",
                                     "qkv_conv_w", "a_log", "dt_bias", "dn_norm_w", "norm2_w")}
    pending = None
    for l in reversed(range(L)):
        sv, W = saved[l], wts[l]
        dmod[l][5] = dgate2
        da_kw = dict(out_dtypes=(BF16,), sides=(sv["a"],),
                     epi=lambda acc, a: (acc * (2.0 * jnp.maximum(a.astype(F32), 0.0)),))
        if pending is None:
            da = _matmul(dm, W["w_down"], "nt", f"d_r{l}", **da_kw)
            g_down = _matmul(sv["r"], dm, "tn", f"g_down{l}")
            late_comm = None
        else:
            da, sib = _matmul(dm, W["w_down"], "nt", f"d_r{l}", comm=_sib_comm(pending), **da_kw)
            parts = [_pair_sum(g, s, f"rs_sum{l + 1}_{k}") for k, (g, s) in enumerate(zip(pending, sib))]
            g_down, lands[:N_EARLY] = _matmul(sv["r"], dm, "tn", f"g_down{l}",
                                              comm=_chip_comm(parts[:N_EARLY], lands[:N_EARLY], l + 1))
            late_comm = _chip_comm(parts[N_EARLY:], lands[N_EARLY:], l + 1)
        dh2 = _matmul(da, W["w_up"], "nt", f"d_h2{l}")
        g_up = _matmul(sv["h2"], da, "tn", f"g_up{l}", split_n=N_DEV).reshape(N_CHIP, 2, D, DFF // N_DEV)
        g_down = _rows_to_shards(g_down)
        early = l == 0
        (dxl, dy), (dmod[l][2], dmod[l][3], dmod[l][4]), (small["norm2_w"][l],) = _rowwise_bwd(
            _f_mid, [(sv["xl"], 0, D), (sv["y"], 0, D)], [modv(l, 2), modv(l, 3), modv(l, 4)], [pv(norm2_w[l])],
            [(dx1, 0, D), (dh2, 0, D)], [F32, BF16], f"mid_b{l}")
        if early:
            dycat, sib0 = _matmul(dy, W["w_out"], "nt", f"d_ycat{l}", comm=_sib_comm([g_up, g_down]))
            part_up, part_down = [_pair_sum(g, s, f"rs_sum0_{k}") for k, g, s in zip((3, 4), (g_up, g_down), sib0)]
        else:
            dycat = _matmul(dy, W["w_out"], "nt", f"d_ycat{l}")
        g_out = _matmul(sv["ycat"], dy, "tn", f"g_out{l}")
        (du3,), _, (small["conv_out_norm_w"][l],) = _rowwise_bwd(
            _f_rms, [(sv["u3"], 0, CC)], [], [pv(conv_out_norm_w[l])], [(dycat, 0, CC)], [BF16], f"convout_b{l}", tile=256)
        du2 = _matmul(du3, W["w_pw2"], "nt", f"d_u2{l}")
        g_pw2 = _matmul(sv["u2"], du3, "tn", f"g_pw2{l}")
        (du1,), _, (small["conv_dw_b"][l], small["conv_ln_w"][l], small["conv_ln_b"][l]) = _rowwise_bwd(
            _f_convmid, [(sv["u1"], 0, CC)], [], sv["cpv"], [(du2, 0, CC)], [F32], f"convmid_b{l}", tile=256)
        du0, small["conv_dw_w"][l] = _dwconv_bwd(sv["u0"], 0, conv_w_full[l], 0, du1, CC, S, f"conv_b{l}", F32)
        (dval, dgate), _, _ = _rowwise_bwd(_f_glu, [(sv["proj"], 0, CC), (sv["proj"], 1, CC)], [], [],
                                           [(du0, 0, CC)], [BF16, BF16], f"glu_b{l}", tile=256)
        (dq_c, dk_c, dv_c, dz, dlg, g_alog, g_dt, small["dn_norm_w"][l]), late = _delta_bwd(
            sv["qkv_c"], sv["proj"], *sv["dvec"], sv["ssave"], dycat, CC, DN, CC, S, f"delta_b{l}", HP, comm=late_comm)
        if late_comm is not None:
            lands[N_EARLY:] = late
        small["a_log"][l], small["dt_bias"][l] = g_alog[0, :H], g_dt[0, :H]
        dqkv, gqw = [], []
        for p, dpc in enumerate((dq_c, dk_c, dv_c)):
            dpart, gw = _dwconv_bwd(sv["proj"], 2 * CC + p * DN, qkv_w_full[l], p * DN, dpc, DN, S,
                                    f"qkvconv_b{l}_{p}", BF16)
            dqkv.append(dpart)
            gqw.append(gw)
        small["qkv_conv_w"][l] = jnp.concatenate(gqw, axis=1)
        (dlog,) = _rowwise(lambda *gs: (sum(gs[1:], gs[0]),), [(dlg[g], 0, LANE) for g in range(dlg.shape[0])], [], [],
                           [(LANE, BF16)], f"dlog{l}")
        pieces = [dval, dgate, *dqkv, dz, dlog]
        dproj = jnp.concatenate(pieces, axis=1)
        if early:
            dh, lands[4:5] = _matmul(dproj, W["w_in"], "nt", f"d_h{l}", comm=_chip_comm([part_down], lands[4:5], 0))
            g_in, lands[3:4] = _matmul(sv["h"], dproj, "tn", f"g_in{l}", comm=_chip_comm([part_up], lands[3:4], 0))
        else:
            dh = _matmul(dproj, W["w_in"], "nt", f"d_h{l}")
            g_in = _matmul(sv["h"], dproj, "tn", f"g_in{l}")
        if l == 0:
            (grad_x,), (dmod[l][0], dmod[l][1]), (small["norm1_w"][l],) = _rowwise_bwd(
                _f_first, [(xf, 0, D)], [modv(l, 0), modv(l, 1)], [pv(norm1_w[l])],
                [(dxl, 0, D), (dh, 0, D)], [F32], f"pre_b{l}")
        else:
            (dx1, dm), (dgate2, dmod[l][0], dmod[l][1]), (small["norm1_w"][l],) = _rowwise_bwd(
                _f_join, [(sv["x1p"], 0, D), (sv["mp"], 0, D)], [modv(l - 1, 5), modv(l, 0), modv(l, 1)],
                [pv(norm1_w[l])], [(dxl, 0, D), (dh, 0, D)], [F32, BF16], f"pre_b{l}")
        pending = [_cols_to_shards(g_in, IN_COLS), _rows_to_shards(g_pw2), _rows_to_shards(g_out), g_up, g_down]
    pending = pending[:N_EARLY]
    sib = _run_comm(_sib_comm(pending), "rs_sib0")
    parts = [_pair_sum(g, s, f"rs_sum0_{k}") for k, (g, s) in enumerate(zip(pending, sib))]
    lands[:N_EARLY] = _run_comm(_chip_comm(parts, lands[:N_EARLY], 0), "rs_chip0")

    small_list = [jnp.stack(small[k]) for k in ("norm1_w", "conv_dw_w", "conv_dw_b", "conv_ln_w", "conv_ln_b",
                                                "conv_out_norm_w", "qkv_conv_w", "a_log", "dt_bias", "dn_norm_w",
                                                "norm2_w")]
    small_list += [dwf.reshape(-1), loss_part[0, :1]]
    small_shapes = [a.shape for a in small_list]
    dmod_local = jnp.concatenate(
        [jnp.concatenate([dmod[l][k].reshape(Bl, D) for k in range(NMOD)], axis=1) for l in range(L)]
        + [dshift_f.reshape(Bl, D), dscale_f.reshape(Bl, D)], axis=1)
    packed8, dmod8 = _all_gather([_pack(small_list), dmod_local], "ag_grads")
    summed = _unpack(_sum_devices(packed8, "sum_small"), small_shapes)
    (g_norm1, g_cdw_full, g_cdb, g_clw, g_clb, g_con, g_qcw_full, g_alog, g_dtb, g_dnw, g_norm2, g_fnw, loss) = summed
    loss = loss.reshape(())
    g_cdw = lax.dynamic_slice_in_dim(g_cdw_full, dev * conv_dw_w.shape[2], conv_dw_w.shape[2], axis=2)
    g_qcw = lax.dynamic_slice_in_dim(g_qcw_full, dev * qkv_conv_w.shape[2], qkv_conv_w.shape[2], axis=2)

    dmod_all = dmod8.reshape(N_DEV * Bl, L * NMOD * D + 2 * D)
    g_w_ada, g_b_ada = [], []
    for l in range(L):
        dm_l = dmod_all[:, l * NMOD * D:(l + 1) * NMOD * D]
        cols = lax.dynamic_slice_in_dim(dm_l, dev * ada_cols, ada_cols, axis=1)
        g_w_ada.append(_matmul(c_act, cols, "tn", f"g_ada{l}", precision=HI, tk=N_DEV * Bl))
    dm_f = dmod_all[:, L * NMOD * D:]
    cols_f = lax.dynamic_slice_in_dim(dm_f, dev * fada_cols, fada_cols, axis=1)
    g_fada_w = _matmul(c_act, cols_f, "tn", "g_fada", precision=HI, tk=N_DEV * Bl)
    g_w_ada = jnp.stack(g_w_ada)
    bsum = _bias_sum(dmod_all, "g_bias")
    g_b_ada = bsum[0, :L * NMOD * D].reshape(L, NMOD * D)
    g_fada_b = bsum[0, L * NMOD * D:]

    big_out = {}
    for k, land in zip(big, lands):
        w_, m_, v_ = wstack[k]
        big_out[k] = _adam_landed(land, w_, m_, v_, f"adam_{k}")
    ada = _adam(g_w_ada, w_ada, m_w_ada, v_w_ada, "adam_w_ada")
    fada = [a[0] for a in _adam(g_fada_w[None], final_ada_w[None], m_final_ada_w[None], v_final_ada_w[None], "adam_fada")]
    sm_names = ["b_ada", "norm1_w", "conv_dw_w", "conv_dw_b", "conv_ln_w", "conv_ln_b", "conv_out_norm_w", "qkv_conv_w",
                "a_log", "dt_bias", "dn_norm_w", "norm2_w", "final_ada_b", "final_norm_w"]
    sm_g = dict(b_ada=g_b_ada, norm1_w=g_norm1, conv_dw_w=g_cdw, conv_dw_b=g_cdb, conv_ln_w=g_clw, conv_ln_b=g_clb,
                conv_out_norm_w=g_con, qkv_conv_w=g_qcw, a_log=g_alog, dt_bias=g_dtb, dn_norm_w=g_dnw, norm2_w=g_norm2,
                final_ada_b=g_fada_b, final_norm_w=g_fnw)
    sm_w = dict(b_ada=(b_ada, m_b_ada, v_b_ada), norm1_w=(norm1_w, m_norm1_w, v_norm1_w),
                conv_dw_w=(conv_dw_w, m_conv_dw_w, v_conv_dw_w), conv_dw_b=(conv_dw_b, m_conv_dw_b, v_conv_dw_b),
                conv_ln_w=(conv_ln_w, m_conv_ln_w, v_conv_ln_w), conv_ln_b=(conv_ln_b, m_conv_ln_b, v_conv_ln_b),
                conv_out_norm_w=(conv_out_norm_w, m_conv_out_norm_w, v_conv_out_norm_w),
                qkv_conv_w=(qkv_conv_w, m_qkv_conv_w, v_qkv_conv_w), a_log=(a_log, m_a_log, v_a_log),
                dt_bias=(dt_bias, m_dt_bias, v_dt_bias), dn_norm_w=(dn_norm_w, m_dn_norm_w, v_dn_norm_w),
                norm2_w=(norm2_w, m_norm2_w, v_norm2_w), final_ada_b=(final_ada_b, m_final_ada_b, v_final_ada_b),
                final_norm_w=(final_norm_w, m_final_norm_w, v_final_norm_w))
    sm_shapes = [sm_w[k][0].shape for k in sm_names]
    sm = _adam(_pack([sm_g[k] for k in sm_names])[None], _pack([sm_w[k][0] for k in sm_names])[None],
               _pack([sm_w[k][1] for k in sm_names])[None], _pack([sm_w[k][2] for k in sm_names])[None], "adam_small")
    sm_d, sm_m, sm_v = (dict(zip(sm_names, _unpack(a[0], sm_shapes))) for a in sm)
    for k in sm_names:
        sm_g[k] = sm_g[k].reshape(sm_w[k][0].shape)

    grads = dict(w_ada=g_w_ada, final_ada_w=g_fada_w, **{k: big_out[k][0] for k in big}, **sm_g)
    deltas = dict(w_ada=ada[0], final_ada_w=fada[0], **{k: big_out[k][1] for k in big}, **sm_d)
    new_m = dict(w_ada=ada[1], final_ada_w=fada[1], **{k: big_out[k][2] for k in big}, **sm_m)
    new_v = dict(w_ada=ada[2], final_ada_w=fada[2], **{k: big_out[k][3] for k in big}, **sm_v)
    order = ["w_ada", "b_ada", "norm1_w", "w_in", "conv_dw_w", "conv_dw_b", "conv_ln_w", "conv_ln_b", "w_pw2",
             "conv_out_norm_w", "qkv_conv_w", "a_log", "dt_bias", "dn_norm_w", "w_out", "norm2_w", "w_up", "w_down",
             "final_ada_w", "final_ada_b", "final_norm_w"]
    return (loss, grad_x.reshape(Bl, S, D), *[grads[k] for k in order], *[deltas[k] for k in order],
            *[new_m[k] for k in order], *[new_v[k] for k in order])
```
